```python
import jax
import jax.numpy as jnp
from jax import lax
import numpy as np

D_MODEL = 4096
BATCH = 4
SEQ = 2048
DEPTH = 4
DEC_BATCH = 8
DEC_SEQ = 1
PAST_LEN = 8192
PAGE_SIZE = 128

N_A_LAYERS = DEPTH // 2
N_B_LAYERS = DEPTH - N_A_LAYERS
H_A = D_MODEL // 128
DK_A = 128
DV_A = 128
QK_DIM_A = H_A * DK_A
V_DIM_A = H_A * DV_A
CONV_W = 4
CONV_DIM = 2 * QK_DIM_A + V_DIM_A
IN_A_DIM = CONV_DIM + V_DIM_A + 2 * H_A
CHUNK = 64
H_B = D_MODEL // 128
D_HEAD_B = 128
B_DIM = H_B * D_HEAD_B
Q_BLOCK = 128
FORGET_BIAS = 3.0
D_FF = 4 * D_MODEL
EPS = 1e-6

kernel_name = 'yoco_gated_deltanet_fox_decoder'


def rmsnorm(x, g):
    xf = x.astype(jnp.float32)
    y = xf * lax.rsqrt(jnp.mean(xf * xf, axis=-1, keepdims=True) + EPS)
    return (y * g.astype(jnp.float32)).astype(x.dtype)


def l2norm(x):
    xf = x.astype(jnp.float32)
    return xf * lax.rsqrt(jnp.sum(xf * xf, axis=-1, keepdims=True) + EPS)


def squared_relu_mlp(x, w_up, w_down):
    h = jax.nn.relu(x @ w_up)
    return (h * h) @ w_down


def causal_short_conv(u, buf, w_conv):
    L = u.shape[1]
    ext = jnp.concatenate([buf.astype(u.dtype), u], axis=1)
    out = ext[:, 0:L] * w_conv[0]
    for i in range(1, CONV_W):
        out = out + ext[:, i:i + L] * w_conv[i]
    return jax.nn.silu(out), ext[:, L:]


def gated_delta_rule(q, k, v, g, beta, s0):
    B, L, H, DK = k.shape
    DV = v.shape[-1]
    C = min(CHUNK, L)
    NC = -(-L // C)
    pad = NC * C - L

    def to_chunks(t):
        t = t.astype(jnp.float32)
        t = jnp.pad(t, [(0, 0), (0, pad)] + [(0, 0)] * (t.ndim - 2))
        t = t.reshape((B, NC, C) + t.shape[2:])
        return jnp.moveaxis(t, 3, 1)

    qc, kc, vc, gc, bc = (to_chunks(t) for t in (q, k, v, g, beta))
    dcum = jnp.cumsum(gc, axis=-1)
    diff = dcum[..., :, None] - dcum[..., None, :]
    tri_incl = jnp.tril(jnp.ones((C, C), dtype=bool))
    tri_strict = jnp.tril(jnp.ones((C, C), dtype=bool), -1)
    decay_incl = jnp.where(tri_incl, jnp.exp(jnp.where(tri_incl, diff, 0.0)), 0.0)
    decay_strict = jnp.where(tri_strict, decay_incl, 0.0)
    a_mat = bc[..., None] * jnp.einsum('bhnid,bhnjd->bhnij', kc, kc) * decay_strict
    rhs = jnp.concatenate([vc * bc[..., None], kc * (bc * jnp.exp(dcum))[..., None]], axis=-1)
    sol = lax.linalg.triangular_solve(a_mat + jnp.eye(C, dtype=jnp.float32), rhs,
                                      left_side=True, lower=True, unit_diagonal=True)
    u, w = sol[..., :DV], sol[..., DV:]
    attn_intra = jnp.einsum('bhnid,bhnjd->bhnij', qc, kc) * decay_incl
    q_dec = qc * jnp.exp(dcum)[..., None]
    k_dec = kc * jnp.exp(dcum[..., -1:] - dcum)[..., None]
    g_last = jnp.exp(dcum[..., -1])

    def step(S, inp):
        u_n, w_n, a_n, qd_n, kd_n, gl_n = inp
        v_new = u_n - jnp.einsum('bhck,bhkv->bhcv', w_n, S)
        o_n = jnp.einsum('bhck,bhkv->bhcv', qd_n, S) + jnp.einsum('bhij,bhjv->bhiv', a_n, v_new)
        S = S * gl_n[..., None, None] + jnp.einsum('bhck,bhcv->bhkv', kd_n, v_new)
        return S, o_n

    xs = tuple(jnp.moveaxis(t, 2, 0) for t in (u, w, attn_intra, q_dec, k_dec, g_last))
    s_final, o = lax.scan(step, s0.astype(jnp.float32), xs)
    o = jnp.transpose(o, (1, 0, 3, 2, 4)).reshape(B, NC * C, H, DV)[:, :L]
    return o, s_final


def gated_deltanet(xn, conv_buf, s0, w_in, w_conv, a_log, dt_bias, o_norm_g, w_out):
    B, L, _ = xn.shape
    proj = xn @ w_in
    qkv, z, a, b = jnp.split(proj, [CONV_DIM, CONV_DIM + V_DIM_A, CONV_DIM + V_DIM_A + H_A], axis=-1)
    qkv, new_buf = causal_short_conv(qkv, conv_buf, w_conv)
    q, k, v = jnp.split(qkv, [QK_DIM_A, 2 * QK_DIM_A], axis=-1)
    q = l2norm(q.reshape(B, L, H_A, DK_A)) * (DK_A ** -0.5)
    k = l2norm(k.reshape(B, L, H_A, DK_A))
    v = v.reshape(B, L, H_A, DV_A)
    g = -jnp.exp(a_log.astype(jnp.float32)) * jax.nn.softplus(a.astype(jnp.float32) + dt_bias.astype(jnp.float32))
    beta = jax.nn.sigmoid(b.astype(jnp.float32))
    o, s_new = gated_delta_rule(q, k, v, g, beta, s0)
    o = rmsnorm(o, o_norm_g) * jax.nn.silu(z.reshape(B, L, H_A, DV_A).astype(jnp.float32))
    y = o.reshape(B, L, V_DIM_A).astype(xn.dtype) @ w_out
    return y, new_buf, s_new.astype(s0.dtype)


def forgetting_attention(q, k, v, cq, ck, q_pos, k_pos):
    B, Lq, H, D = q.shape
    blk = min(Q_BLOCK, Lq)
    nb = -(-Lq // blk)
    pad = nb * blk - Lq
    kf = k.astype(jnp.float32)
    vf = v.astype(jnp.float32)
    ck_t = jnp.transpose(ck, (0, 2, 1))
    qb = jnp.pad(q.astype(jnp.float32), ((0, 0), (0, pad), (0, 0), (0, 0))).reshape(B, nb, blk, H, D).swapaxes(0, 1)
    cqb = jnp.pad(cq, ((0, 0), (0, pad), (0, 0))).reshape(B, nb, blk, H).swapaxes(0, 1)
    pb = jnp.pad(q_pos, (0, pad), mode='edge').reshape(nb, blk)
    scale = D ** -0.5

    def one_block(args):
        qi, cqi, pi = args
        s = jnp.einsum('bqhd,bkhd->bhqk', qi, kf) * scale
        s = s + jnp.transpose(cqi, (0, 2, 1))[..., None] - ck_t[:, :, None, :]
        s = jnp.where(pi[:, None] >= k_pos[None, :], s, -jnp.inf)
        p = jax.nn.softmax(s, axis=-1)
        return jnp.einsum('bhqk,bkhd->bqhd', p, vf)

    o = lax.map(one_block, (qb, cqb, pb))
    o = o.swapaxes(0, 1).reshape(B, nb * blk, H, D)[:, :Lq]
    return o.astype(q.dtype)


def trunk(x, conv_state, delta_state, past_k, past_v, past_logf,
          norm_mix_g, norm_mlp_g, w_up, w_down,
          a_w_in, a_w_conv, a_log, a_dt_bias, a_o_norm_g, a_w_out,
          kv_norm_g, w_k, w_v, w_f, b_f, b_w_q, b_w_o, final_norm_g):
    B, L, _ = x.shape
    past_len = past_k.shape[1]
    h = x
    new_conv, new_delta = [], []
    for l in range(N_A_LAYERS):
        y, cb, s = gated_deltanet(rmsnorm(h, norm_mix_g[l]), conv_state[l], delta_state[l],
                                  a_w_in[l], a_w_conv[l], a_log[l], a_dt_bias[l], a_o_norm_g[l], a_w_out[l])
        h = h + y
        h = h + squared_relu_mlp(rmsnorm(h, norm_mlp_g[l]), w_up[l], w_down[l])
        new_conv.append(cb)
        new_delta.append(s)
    hn = rmsnorm(h, kv_norm_g)
    k_new = (hn @ w_k).reshape(B, L, H_B, D_HEAD_B)
    v_new = (hn @ w_v).reshape(B, L, H_B, D_HEAD_B)
    logf_new = jax.nn.log_sigmoid((hn @ w_f + b_f).astype(jnp.float32))
    k_all = jnp.concatenate([past_k.astype(k_new.dtype), k_new], axis=1)
    v_all = jnp.concatenate([past_v.astype(v_new.dtype), v_new], axis=1)
    cum = jnp.cumsum(jnp.concatenate([past_logf.astype(jnp.float32), logf_new], axis=1), axis=1)
    cq = cum[:, past_len:]
    k_pos = jnp.arange(past_len + L, dtype=jnp.int32)
    q_pos = past_len + jnp.arange(L, dtype=jnp.int32)
    for j in range(N_B_LAYERS):
        l = N_A_LAYERS + j
        q = (rmsnorm(h, norm_mix_g[l]) @ b_w_q[j]).reshape(B, L, H_B, D_HEAD_B)
        o = forgetting_attention(q, k_all, v_all, cq, cum, q_pos, k_pos)
        h = h + o.reshape(B, L, B_DIM) @ b_w_o[j]
        h = h + squared_relu_mlp(rmsnorm(h, norm_mlp_g[l]), w_up[l], w_down[l])
    y = rmsnorm(h, final_norm_g)
    return (y, jnp.stack(new_delta), jnp.stack(new_conv), k_new, v_new, logf_new.astype(past_logf.dtype))


def setup_inputs(seed: int = 0) -> dict:
    key = jax.random.key(seed)
    ks = iter(jax.random.split(key, 40))
    f32 = jnp.float32

    def nrm(shape, scale):
        return jax.random.normal(next(ks), shape, f32) * scale

    n_pages = PAST_LEN // PAGE_SIZE
    n_used = DEC_BATCH * n_pages
    n_phys = (5 * n_used + 3) // 4
    x_prompt = nrm((BATCH, SEQ, D_MODEL), 1.0)
    x_sample = nrm((DEC_BATCH, DEC_SEQ, D_MODEL), 1.0)
    cache_k = nrm((n_phys, PAGE_SIZE, H_B, D_HEAD_B), 1.0)
    cache_v = nrm((n_phys, PAGE_SIZE, H_B, D_HEAD_B), 1.0)
    cache_logf = jax.nn.log_sigmoid(FORGET_BIAS + nrm((n_phys, PAGE_SIZE, H_B), 1.0))
    state_delta = nrm((N_A_LAYERS, DEC_BATCH, H_A, DK_A, DV_A), DK_A ** -0.5)
    state_conv = nrm((N_A_LAYERS, DEC_BATCH, CONV_W - 1, CONV_DIM), 1.0)
    page_table = jax.random.permutation(next(ks), n_phys)[:n_used].reshape(DEC_BATCH, n_pages).astype(jnp.int32)
    norm_mix_g = 1.0 + nrm((DEPTH, D_MODEL), 0.02)
    norm_mlp_g = 1.0 + nrm((DEPTH, D_MODEL), 0.02)
    w_up = nrm((DEPTH, D_MODEL, D_FF), D_MODEL ** -0.5)
    w_down = nrm((DEPTH, D_FF, D_MODEL), D_FF ** -0.5)
    a_w_in = nrm((N_A_LAYERS, D_MODEL, IN_A_DIM), D_MODEL ** -0.5)
    a_w_conv = nrm((N_A_LAYERS, CONV_W, CONV_DIM), 0.5)
    a_log = jnp.log(jax.random.uniform(next(ks), (N_A_LAYERS, H_A), f32, 1.0, 16.0))
    dt = jnp.exp(jax.random.uniform(next(ks), (N_A_LAYERS, H_A), f32, float(np.log(1e-3)), float(np.log(1e-1))))
    a_dt_bias = jnp.log(jnp.expm1(dt))
    a_o_norm_g = 1.0 + nrm((N_A_LAYERS, DV_A), 0.02)
    a_w_out = nrm((N_A_LAYERS, V_DIM_A, D_MODEL), V_DIM_A ** -0.5)
    kv_norm_g = 1.0 + nrm((D_MODEL,), 0.02)
    w_k = nrm((D_MODEL, B_DIM), D_MODEL ** -0.5)
    w_v = nrm((D_MODEL, B_DIM), D_MODEL ** -0.5)
    w_f = nrm((D_MODEL, H_B), 0.5 * D_MODEL ** -0.5)
    b_f = FORGET_BIAS + nrm((H_B,), 0.1)
    b_w_q = nrm((N_B_LAYERS, D_MODEL, B_DIM), D_MODEL ** -0.5)
    b_w_o = nrm((N_B_LAYERS, B_DIM, D_MODEL), B_DIM ** -0.5)
    final_norm_g = 1.0 + nrm((D_MODEL,), 0.02)
    return {'x_prompt': x_prompt, 'x_sample': x_sample, 'cache_k': cache_k, 'cache_v': cache_v,
            'cache_logf': cache_logf, 'state_delta': state_delta, 'state_conv': state_conv,
            'page_table': page_table, 'norm_mix_g': norm_mix_g, 'norm_mlp_g': norm_mlp_g,
            'w_up': w_up, 'w_down': w_down, 'a_w_in': a_w_in, 'a_w_conv': a_w_conv, 'a_log': a_log,
            'a_dt_bias': a_dt_bias, 'a_o_norm_g': a_o_norm_g, 'a_w_out': a_w_out, 'kv_norm_g': kv_norm_g,
            'w_k': w_k, 'w_v': w_v, 'w_f': w_f, 'b_f': b_f, 'b_w_q': b_w_q, 'b_w_o': b_w_o,
            'final_norm_g': final_norm_g}


def reference(x_prompt, x_sample, cache_k, cache_v, cache_logf, state_delta, state_conv, page_table,
              norm_mix_g, norm_mlp_g, w_up, w_down, a_w_in, a_w_conv, a_log, a_dt_bias, a_o_norm_g, a_w_out,
              kv_norm_g, w_k, w_v, w_f, b_f, b_w_q, b_w_o, final_norm_g):
    weights = (norm_mix_g, norm_mlp_g, w_up, w_down, a_w_in, a_w_conv, a_log, a_dt_bias, a_o_norm_g, a_w_out,
               kv_norm_g, w_k, w_v, w_f, b_f, b_w_q, b_w_o, final_norm_g)
    n_p = x_prompt.shape[0]
    zero_conv = jnp.zeros((N_A_LAYERS, n_p, CONV_W - 1, CONV_DIM), state_conv.dtype)
    zero_delta = jnp.zeros((N_A_LAYERS, n_p, H_A, DK_A, DV_A), state_delta.dtype)
    empty_k = jnp.zeros((n_p, 0, H_B, D_HEAD_B), cache_k.dtype)
    empty_v = jnp.zeros((n_p, 0, H_B, D_HEAD_B), cache_v.dtype)
    empty_logf = jnp.zeros((n_p, 0, H_B), cache_logf.dtype)
    y_p, delta_p, conv_p, k_p, v_p, logf_p = trunk(x_prompt, zero_conv, zero_delta, empty_k, empty_v, empty_logf, *weights)
    n_s, n_pages = page_table.shape
    past_len = n_pages * PAGE_SIZE
    past_k = cache_k[page_table].reshape(n_s, past_len, H_B, D_HEAD_B)
    past_v = cache_v[page_table].reshape(n_s, past_len, H_B, D_HEAD_B)
    past_logf = cache_logf[page_table].reshape(n_s, past_len, H_B)
    y_s, delta_s, conv_s, k_s, v_s, logf_s = trunk(x_sample, state_conv, state_delta, past_k, past_v, past_logf, *weights)
    return (y_p, y_s, delta_p, conv_p, k_p, v_p, logf_p, delta_s, conv_s, k_s, v_s, logf_s)
```

```python
import functools

import jax
import jax.numpy as jnp
from jax import lax
from jax.experimental import pallas as pl
from jax.experimental.pallas import tpu as pltpu

F32 = jnp.float32
BF16 = jnp.bfloat16

EPS = 1e-6
HEAD = 128
CHUNK = 64
CONV_W = 4
PAGE = 128
NEG = -1e30
V7X_VMEM_LIMIT = 56 * 1024 * 1024


def _cparams(sem):
    return pltpu.CompilerParams(dimension_semantics=sem, vmem_limit_bytes=V7X_VMEM_LIMIT)


def _rms_body(x_ref, g_ref, o_ref):
    x = x_ref[...]
    ms = jnp.mean(x * x, axis=-1, keepdims=True)
    o_ref[...] = (x * lax.rsqrt(ms + EPS) * g_ref[...]).astype(o_ref.dtype)


def rmsnorm(x, g, out_dtype=BF16):
    m, d = x.shape
    tm = min(m, 256)
    return pl.pallas_call(
        _rms_body,
        grid=(m // tm,),
        in_specs=[pl.BlockSpec((tm, d), lambda i: (i, 0)),
                  pl.BlockSpec((1, d), lambda i: (0, 0))],
        out_specs=pl.BlockSpec((tm, d), lambda i: (i, 0)),
        out_shape=jax.ShapeDtypeStruct((m, d), out_dtype),
        compiler_params=_cparams(("parallel",)),
        name="rmsnorm",
    )(x, g.reshape(1, d))


def _mm_body(*refs, nk, relu2, has_res, n_out):
    a_ref, w_ref = refs[0], refs[1]
    pos = 2
    r_ref = None
    if has_res:
        r_ref = refs[pos]
        pos += 1
    o_refs = refs[pos:pos + n_out]
    scr = refs[pos + n_out:]

    def finish(acc):
        if relu2:
            acc = jnp.maximum(acc, 0.0)
            acc = acc * acc
        if has_res:
            acc = r_ref[...] + acc
        for o_ref in o_refs:
            o_ref[...] = acc.astype(o_ref.dtype)

    part = jnp.dot(a_ref[...], w_ref[...], preferred_element_type=F32)
    if nk == 1:
        finish(part)
    else:
        acc_ref = scr[0]
        k = pl.program_id(2)

        @pl.when(k == 0)
        def _():
            acc_ref[...] = part

        @pl.when(jnp.logical_and(k > 0, k < nk - 1))
        def _():
            acc_ref[...] += part

        @pl.when(k == nk - 1)
        def _():
            finish(acc_ref[...] + part)


def matmul(a, w, out_dtypes=(F32,), res=None, relu2=False, tm=1024, tn=1024, tk=2048):
    m, kd = a.shape
    n = w.shape[1]
    tm, tn, tk = min(tm, m), min(tn, n), min(tk, kd)
    nk = kd // tk
    in_specs = [pl.BlockSpec((tm, tk), lambda i, j, k: (i, k)),
                pl.BlockSpec((tk, tn), lambda i, j, k: (k, j))]
    args = [a, w]
    if res is not None:
        in_specs.append(pl.BlockSpec((tm, tn), lambda i, j, k: (i, j)))
        args.append(res)
    out = pl.pallas_call(
        functools.partial(_mm_body, nk=nk, relu2=relu2, has_res=res is not None,
                          n_out=len(out_dtypes)),
        grid=(m // tm, n // tn, nk),
        in_specs=in_specs,
        out_specs=[pl.BlockSpec((tm, tn), lambda i, j, k: (i, j)) for _ in out_dtypes],
        out_shape=[jax.ShapeDtypeStruct((m, n), dt) for dt in out_dtypes],
        scratch_shapes=[pltpu.VMEM((tm, tn), F32)] if nk > 1 else [],
        compiler_params=_cparams(("parallel", "parallel", "arbitrary")),
        name="matmul",
    )(*args)
    return out[0] if len(out_dtypes) == 1 else tuple(out)


def _softplus(x):
    return jnp.maximum(x, 0.0) + jnp.log1p(jnp.exp(-jnp.abs(x)))


def _gates_body(x_ref, alog_ref, dt_ref, o_ref, *, nh):
    x = x_ref[...]
    lane = lax.broadcasted_iota(jnp.int32, x.shape, 1)
    g = -jnp.exp(alog_ref[...]) * _softplus(x + dt_ref[...])
    beta = jax.nn.sigmoid(x)
    o_ref[...] = jnp.where(lane < nh, g, beta)


def gdn_gates(ab, a_log, dt_bias):
    m, w = ab.shape
    nh = a_log.shape[0]
    tm = min(m, 1024)
    pad = lambda v: jnp.pad(v.astype(F32), (0, w - nh)).reshape(1, w)
    return pl.pallas_call(
        functools.partial(_gates_body, nh=nh),
        grid=(m // tm,),
        in_specs=[pl.BlockSpec((tm, w), lambda i: (i, 0)),
                  pl.BlockSpec((1, w), lambda i: (0, 0)),
                  pl.BlockSpec((1, w), lambda i: (0, 0))],
        out_specs=pl.BlockSpec((tm, w), lambda i: (i, 0)),
        out_shape=jax.ShapeDtypeStruct((m, w), F32),
        compiler_params=_cparams(("parallel",)),
        name="gdn_gates",
    )(ab, pad(a_log), pad(dt_bias))


def _logsig_body(x_ref, b_ref, o_ref):
    y = x_ref[...] + b_ref[...]
    o_ref[...] = -_softplus(-y)


def bias_log_sigmoid(x, b):
    m, w = x.shape
    tm = min(m, 1024)
    return pl.pallas_call(
        _logsig_body,
        grid=(m // tm,),
        in_specs=[pl.BlockSpec((tm, w), lambda i: (i, 0)),
                  pl.BlockSpec((1, w), lambda i: (0, 0))],
        out_specs=pl.BlockSpec((tm, w), lambda i: (i, 0)),
        out_shape=jax.ShapeDtypeStruct((m, w), F32),
        compiler_params=_cparams(("parallel",)),
        name="bias_log_sigmoid",
    )(x, jnp.pad(b.astype(F32), (0, w - b.shape[0])).reshape(1, w))


def _l2_heads(y, o_ref, idx, qscale):
    for j in range(y.shape[-1] // HEAD):
        ys = y[..., j * HEAD:(j + 1) * HEAD]
        ss = jnp.sum(ys * ys, axis=-1, keepdims=True)
        o_ref[idx + (slice(j * HEAD, (j + 1) * HEAD),)] = ys * lax.rsqrt(ss + EPS) * qscale


def _conv_body(x_ref, buf_ref, w_ref, o_ref, ext_ref, *, tl, n_q_tiles):
    c = pl.program_id(1)
    t = pl.program_id(2)

    @pl.when(t == 0)
    def _():
        ext_ref[0:8, :] = buf_ref[0]

    u = x_ref[0]
    ext_ref[8:8 + tl, :] = u
    w = w_ref[...]
    acc = ext_ref[5:5 + tl, :] * w[0:1]
    acc = acc + ext_ref[6:6 + tl, :] * w[1:2]
    acc = acc + ext_ref[7:7 + tl, :] * w[2:3]
    acc = acc + u * w[3:4]
    y = acc * jax.nn.sigmoid(acc)

    @pl.when(c < 2 * n_q_tiles)
    def _():
        qscale = jnp.where(c < n_q_tiles, HEAD ** -0.5, 1.0).astype(F32)
        _l2_heads(y, o_ref, (0, slice(None)), qscale)

    @pl.when(c >= 2 * n_q_tiles)
    def _():
        o_ref[0] = y

    ext_ref[0:8, :] = ext_ref[tl:tl + 8, :]


def conv_prompt(proj, buf8, w_conv, qk_dim, tl=512, tc=512):
    b, l, _ = proj.shape
    cd = w_conv.shape[1]
    tl, tc = min(tl, l), min(tc, qk_dim)
    return pl.pallas_call(
        functools.partial(_conv_body, tl=tl, n_q_tiles=qk_dim // tc),
        grid=(b, cd // tc, l // tl),
        in_specs=[pl.BlockSpec((1, tl, tc), lambda i, c, t: (i, t, c)),
                  pl.BlockSpec((1, 8, tc), lambda i, c, t: (i, 0, c)),
                  pl.BlockSpec((CONV_W, tc), lambda i, c, t: (0, c))],
        out_specs=pl.BlockSpec((1, tl, tc), lambda i, c, t: (i, t, c)),
        out_shape=jax.ShapeDtypeStruct((b, l, cd), F32),
        scratch_shapes=[pltpu.VMEM((tl + 8, tc), F32)],
        compiler_params=_cparams(("parallel", "parallel", "arbitrary")),
        name="conv_prompt",
    )(proj, buf8, w_conv)


def _conv_dec_body(u_ref, buf_ref, w_ref, o_ref, *, n_q_tiles):
    c = pl.program_id(0)
    w = w_ref[...]
    acc = buf_ref[:, 0, :] * w[0:1]
    acc = acc + buf_ref[:, 1, :] * w[1:2]
    acc = acc + buf_ref[:, 2, :] * w[2:3]
    acc = acc + u_ref[...] * w[3:4]
    y = acc * jax.nn.sigmoid(acc)

    @pl.when(c < 2 * n_q_tiles)
    def _():
        qscale = jnp.where(c < n_q_tiles, HEAD ** -0.5, 1.0).astype(F32)
        _l2_heads(y, o_ref, (slice(None),), qscale)

    @pl.when(c >= 2 * n_q_tiles)
    def _():
        o_ref[...] = y


def conv_decode(proj, buf, w_conv, qk_dim, tc=512):
    b = proj.shape[0]
    cd = w_conv.shape[1]
    tc = min(tc, qk_dim)
    return pl.pallas_call(
        functools.partial(_conv_dec_body, n_q_tiles=qk_dim // tc),
        grid=(cd // tc,),
        in_specs=[pl.BlockSpec((b, tc), lambda c: (0, c)),
                  pl.BlockSpec((b, CONV_W - 1, tc), lambda c: (0, 0, c)),
                  pl.BlockSpec((CONV_W, tc), lambda c: (0, c))],
        out_specs=pl.BlockSpec((b, tc), lambda c: (0, c)),
        out_shape=jax.ShapeDtypeStruct((b, cd), F32),
        compiler_params=_cparams(("parallel",)),
        name="conv_decode",
    )(proj, buf, w_conv)


def _split(x):
    hi = x.astype(BF16)
    lo = (x - hi.astype(F32)).astype(BF16)
    return hi, lo


def _dot(a, b, dims=(((1,), (0,)), ((), ()))):
    return lax.dot_general(a, b, dims, preferred_element_type=F32)


_NT = (((1,), (1,)), ((), ()))
_TN = (((0,), (0,)), ((), ()))


def _dot1(a, b, dims=(((1,), (0,)), ((), ()))):
    return _dot(a.astype(BF16), b.astype(BF16), dims)


def _dot3(a, b, dims=(((1,), (0,)), ((), ()))):
    ah, al = _split(a)
    bh, bl = _split(b)
    return _dot(ah, bh, dims) + (_dot(ah, bl, dims) + _dot(al, bh, dims))


def _unit_lower_inverse(a_strict, ii, jj):
    c = a_strict.shape[0]
    eye = (ii == jj).astype(F32)
    m = None
    s = 1
    k = 0
    while s < c:
        off = jnp.logical_and((ii >> (k + 1)) == (jj >> (k + 1)),
                              jnp.logical_and(((ii >> k) & 1) == 1, ((jj >> k) & 1) == 0))
        a_off = jnp.where(off, a_strict, 0.0)
        if m is None:
            m = eye - a_off
        else:
            m = m - _dot3(m, _dot3(a_off, m))
        s *= 2
        k += 1
    return m


def _delta_body(q_ref, k_ref, v_ref, z_ref, g_ref, b_ref, s0_ref, gn_ref, o_ref, sf_ref, s_ref,
                *, hb, tl):
    t = pl.program_id(2)
    nt = pl.num_programs(2)
    c = CHUNK
    npc = tl // c

    @pl.when(t == 0)
    def _():
        s_ref[...] = s0_ref[0]

    ii = lax.broadcasted_iota(jnp.int32, (c, c), 0)
    jj = lax.broadcasted_iota(jnp.int32, (c, c), 1)
    eye = ii == jj
    tril = ii >= jj
    stril = ii > jj
    gn = gn_ref[...]

    def col_of(row):
        return jnp.sum(jnp.where(eye, jnp.broadcast_to(row, (c, c)), 0.0), axis=1, keepdims=True)

    for hh in range(hb):
        hs = slice(hh * HEAD, (hh + 1) * HEAD)
        s_mat = s_ref[hh]
        for j in range(npc):
            rs = slice(j * c, (j + 1) * c)
            q = q_ref[0, rs, hs]
            k = k_ref[0, rs, hs]
            v = v_ref[0, rs, hs]
            ci = t * npc + j
            g_row = g_ref[0, hh, pl.ds(ci, 1), :]
            b_row = b_ref[0, hh, pl.ds(ci, 1), :]
            dcum_col = jnp.sum(jnp.where(tril, jnp.broadcast_to(g_row, (c, c)), 0.0),
                               axis=1, keepdims=True)
            dcum_row = jnp.sum(jnp.where(eye, jnp.broadcast_to(dcum_col, (c, c)), 0.0),
                               axis=0, keepdims=True)
            b_col = col_of(b_row)
            diff = dcum_col - dcum_row
            dec = jnp.where(tril, jnp.exp(jnp.where(tril, diff, 0.0)), 0.0)
            dec_s = jnp.where(stril, dec, 0.0)
            a_mat = b_col * _dot3(k, k, _NT) * dec_s
            e_col = jnp.exp(dcum_col)
            rhs = jnp.concatenate([v * b_col, k * (b_col * e_col)], axis=1)
            sol = _dot3(_unit_lower_inverse(a_mat, ii, jj), rhs)
            u = sol[:, :HEAD]
            w = sol[:, HEAD:]
            attn = _dot1(q, k, _NT) * dec
            q_dec = q * e_col
            d_last = dcum_col[c - 1:c, :]
            k_dec = k * jnp.exp(d_last - dcum_col)
            g_last = jnp.exp(d_last)
            wq_s = _dot1(jnp.concatenate([w, q_dec], axis=0), s_mat)
            v_new = u - wq_s[:c]
            o = wq_s[c:] + _dot1(attn, v_new)
            s_mat = s_mat * g_last + _dot1(k_dec, v_new, _TN)
            ms = jnp.mean(o * o, axis=-1, keepdims=True)
            on = o * lax.rsqrt(ms + EPS) * gn
            z = z_ref[0, rs, hs]
            o_ref[0, rs, hs] = (on * (z * jax.nn.sigmoid(z))).astype(o_ref.dtype)
        s_ref[hh] = s_mat

    @pl.when(t == nt - 1)
    def _():
        sf_ref[0] = s_ref[...]


def delta_prompt(qkv, proj, g, beta, s0, gnorm, n_heads, hb=2, tl=256):
    b, l, cd = qkv.shape
    tl = min(tl, l)
    nhg = n_heads // hb
    wb = hb * HEAD
    nc = l // CHUNK
    z_off = cd // wb
    return pl.pallas_call(
        functools.partial(_delta_body, hb=hb, tl=tl),
        grid=(b, nhg, l // tl),
        in_specs=[pl.BlockSpec((1, tl, wb), lambda i, h, t: (i, t, h)),
                  pl.BlockSpec((1, tl, wb), lambda i, h, t: (i, t, nhg + h)),
                  pl.BlockSpec((1, tl, wb), lambda i, h, t: (i, t, 2 * nhg + h)),
                  pl.BlockSpec((1, tl, wb), lambda i, h, t: (i, t, z_off + h)),
                  pl.BlockSpec((1, hb, nc, CHUNK), lambda i, h, t: (i, h, 0, 0)),
                  pl.BlockSpec((1, hb, nc, CHUNK), lambda i, h, t: (i, h, 0, 0)),
                  pl.BlockSpec((1, hb, HEAD, HEAD), lambda i, h, t: (i, h, 0, 0)),
                  pl.BlockSpec((1, HEAD), lambda i, h, t: (0, 0))],
        out_specs=[pl.BlockSpec((1, tl, wb), lambda i, h, t: (i, t, h)),
                   pl.BlockSpec((1, hb, HEAD, HEAD), lambda i, h, t: (i, h, 0, 0))],
        out_shape=[jax.ShapeDtypeStruct((b, l, n_heads * HEAD), BF16),
                   jax.ShapeDtypeStruct((b, n_heads, HEAD, HEAD), F32)],
        scratch_shapes=[pltpu.VMEM((hb, HEAD, HEAD), F32)],
        compiler_params=_cparams(("parallel", "parallel", "arbitrary")),
        name="delta_prompt",
    )(qkv, qkv, qkv, proj, g, beta, s0, gnorm.reshape(1, HEAD))


def _delta_dec_body(q_ref, k_ref, v_ref, z_ref, g_ref, b_ref, s0_ref, gn_ref, o_ref, sf_ref, *, nh):
    ii = lax.broadcasted_iota(jnp.int32, (HEAD, HEAD), 0)
    jj = lax.broadcasted_iota(jnp.int32, (HEAD, HEAD), 1)
    eye = ii == jj
    gn = gn_ref[...]

    def col_of(row):
        return jnp.sum(jnp.where(eye, jnp.broadcast_to(row, (HEAD, HEAD)), 0.0), axis=1, keepdims=True)

    def head(h, carry):
        q = q_ref[0, pl.ds(h, 1), :]
        k = k_ref[0, pl.ds(h, 1), :]
        v = v_ref[0, pl.ds(h, 1), :]
        z = z_ref[0, pl.ds(h, 1), :]
        eg = jnp.exp(g_ref[0, pl.ds(h, 1), :])
        beta = b_ref[0, pl.ds(h, 1), :]
        s_mat = s0_ref[0, h]
        k_col = col_of(k)
        ks = jnp.sum(k_col * s_mat, axis=0, keepdims=True)
        qs = jnp.sum(col_of(q) * s_mat, axis=0, keepdims=True)
        v_new = v * beta - (beta * eg) * ks
        qk = jnp.sum(q * k, axis=-1, keepdims=True)
        o = eg * qs + qk * v_new
        sf_ref[0, h] = s_mat * eg + k_col * v_new
        ms = jnp.mean(o * o, axis=-1, keepdims=True)
        on = o * lax.rsqrt(ms + EPS) * gn
        o_ref[0, pl.ds(h, 1), :] = (on * (z * jax.nn.sigmoid(z))).astype(o_ref.dtype)
        return carry

    lax.fori_loop(0, nh, head, 0)


def delta_decode(q, k, v, z, g_rep, b_rep, s0, gnorm):
    b, nh, _ = q.shape
    vec = pl.BlockSpec((1, nh, HEAD), lambda i: (i, 0, 0))
    st = pl.BlockSpec((1, nh, HEAD, HEAD), lambda i: (i, 0, 0, 0))
    return pl.pallas_call(
        functools.partial(_delta_dec_body, nh=nh),
        grid=(b,),
        in_specs=[vec, vec, vec, vec, vec, vec, st, pl.BlockSpec((1, HEAD), lambda i: (0, 0))],
        out_specs=[vec, st],
        out_shape=[jax.ShapeDtypeStruct((b, nh, HEAD), F32),
                   jax.ShapeDtypeStruct((b, nh, HEAD, HEAD), F32)],
        compiler_params=_cparams(("parallel",)),
        name="delta_decode",
    )(q, k, v, z, g_rep, b_rep, s0, gnorm.reshape(1, HEAD))


def _cumsum_body(pt_ref, x_ref, o_ref, tot_ref, carry_ref):
    del pt_ref
    p = pl.program_id(1)
    npg = pl.num_programs(1)

    @pl.when(p == 0)
    def _():
        carry_ref[...] = jnp.zeros_like(carry_ref)

    x = x_ref[0]
    n = x.shape[0]
    tri = (lax.broadcasted_iota(jnp.int32, (n, n), 0)
           >= lax.broadcasted_iota(jnp.int32, (n, n), 1)).astype(BF16)
    hi = x.astype(BF16)
    r1 = x - hi.astype(F32)
    mid = r1.astype(BF16)
    lo = (r1 - mid.astype(F32)).astype(BF16)
    cs = _dot(tri, hi) + (_dot(tri, mid) + _dot(tri, lo))
    out = cs + carry_ref[...]
    o_ref[0] = out
    carry_ref[...] = out[n - 1:n, :]

    @pl.when(p == npg - 1)
    def _():
        tot_ref[0] = out[n - 1:n, :]


def paged_cumsum(pages, table):
    b, npg = table.shape
    _, pg, h = pages.shape
    return pl.pallas_call(
        _cumsum_body,
        grid_spec=pltpu.PrefetchScalarGridSpec(
            num_scalar_prefetch=1,
            grid=(b, npg),
            in_specs=[pl.BlockSpec((1, pg, h), lambda i, p, pt: (pt[i, p], 0, 0))],
            out_specs=[pl.BlockSpec((1, pg, h), lambda i, p, pt: (i, p, 0)),
                       pl.BlockSpec((1, 1, h), lambda i, p, pt: (i, 0, 0))],
            scratch_shapes=[pltpu.VMEM((1, h), F32)]),
        out_shape=[jax.ShapeDtypeStruct((b, npg * pg, h), F32),
                   jax.ShapeDtypeStruct((b, 1, h), F32)],
        compiler_params=_cparams(("parallel", "arbitrary")),
        name="paged_cumsum",
    )(table, pages)


def _flash_body(q_ref, k_ref, v_ref, cum_ref, ckt_ref, o_ref, m_ref, l_ref, acc_ref, cq_ref,
                *, tq, tk, scale):
    h = pl.program_id(1)
    qi = pl.program_id(2)
    ki = pl.program_id(3)
    nk = pl.num_programs(3)

    @pl.when(ki == 0)
    def _():
        m_ref[...] = jnp.full_like(m_ref, NEG)
        l_ref[...] = jnp.zeros_like(l_ref)
        acc_ref[...] = jnp.zeros_like(acc_ref)
        cum = cum_ref[0]
        lane = lax.broadcasted_iota(jnp.int32, cum.shape, 1)
        cq_ref[...] = jnp.sum(jnp.where(lane == h, cum, 0.0), axis=1, keepdims=True)

    @pl.when(ki * tk <= qi * tq + (tq - 1))
    def _():
        s = _dot(q_ref[0], k_ref[0], _NT) * scale
        s = s + cq_ref[...] - ckt_ref[0, 0]
        qpos = qi * tq + lax.broadcasted_iota(jnp.int32, (tq, tk), 0)
        kpos = ki * tk + lax.broadcasted_iota(jnp.int32, (tq, tk), 1)
        s = jnp.where(qpos >= kpos, s, NEG)
        m_prev = m_ref[...]
        m_new = jnp.maximum(m_prev, jnp.max(s, axis=1, keepdims=True))
        alpha = jnp.exp(m_prev - m_new)
        p = jnp.exp(s - m_new)
        l_ref[...] = alpha * l_ref[...] + jnp.sum(p, axis=1, keepdims=True)
        acc_ref[...] = alpha * acc_ref[...] + _dot(p.astype(BF16), v_ref[0])
        m_ref[...] = m_new

    @pl.when(ki == nk - 1)
    def _():
        o_ref[0] = (acc_ref[...] / l_ref[...]).astype(o_ref.dtype)


def fox_prompt(q, k, v, cum, cum_t, n_heads, tq=1024, tk=1024):
    b, l, _ = q.shape
    tq, tk = min(tq, l), min(tk, l)
    kv_map = lambda i, h, qi, ki: (i, jnp.minimum(ki, (qi * tq + tq - 1) // tk), h)
    return pl.pallas_call(
        functools.partial(_flash_body, tq=tq, tk=tk, scale=HEAD ** -0.5),
        grid=(b, n_heads, l // tq, l // tk),
        in_specs=[pl.BlockSpec((1, tq, HEAD), lambda i, h, qi, ki: (i, qi, h)),
                  pl.BlockSpec((1, tk, HEAD), kv_map),
                  pl.BlockSpec((1, tk, HEAD), kv_map),
                  pl.BlockSpec((1, tq, n_heads), lambda i, h, qi, ki: (i, qi, 0)),
                  pl.BlockSpec((1, 1, 1, tk),
                               lambda i, h, qi, ki: (i, h, 0, jnp.minimum(ki, (qi * tq + tq - 1) // tk)))],
        out_specs=pl.BlockSpec((1, tq, HEAD), lambda i, h, qi, ki: (i, qi, h)),
        out_shape=jax.ShapeDtypeStruct(q.shape, BF16),
        scratch_shapes=[pltpu.VMEM((tq, 1), F32), pltpu.VMEM((tq, 1), F32),
                        pltpu.VMEM((tq, HEAD), F32), pltpu.VMEM((tq, 1), F32)],
        compiler_params=_cparams(("parallel", "parallel", "parallel", "arbitrary")),
        name="fox_prompt",
    )(q, k, v, cum, cum_t)


def _expand3(x, et):
    hi = x.astype(BF16).astype(F32)
    r1 = x - hi
    mid = r1.astype(BF16).astype(F32)
    lo = (r1 - mid).astype(BF16).astype(F32)
    row = lax.broadcasted_iota(jnp.int32, (16, x.shape[1]), 0)
    st = jnp.where(row == 0, hi, jnp.where(row == 1, mid, jnp.where(row == 2, lo, 0.0)))
    ex = _dot(st.astype(BF16), et)
    return ex[0:1] + (ex[1:2] + ex[2:3])


def _sum8(x):
    acc = x[0:8]
    for r in range(1, x.shape[0] // 8):
        acc = acc + x[r * 8:(r + 1) * 8]
    return acc


def _fox_dec_body(pt_ref, q_ref, kc_ref, vc_ref, ck_ref, tot_ref, lf_ref, kn_ref, vn_ref,
                  e_ref, et_ref, o_ref, m_ref, l_ref, acc_ref, *, scale):
    del pt_ref
    p = pl.program_id(1)
    npg = pl.num_programs(1)

    @pl.when(p == 0)
    def _():
        m_ref[...] = jnp.full_like(m_ref, NEG)
        l_ref[...] = jnp.zeros_like(l_ref)
        acc_ref[...] = jnp.zeros_like(acc_ref)

    q = q_ref[0]
    e = e_ref[...]
    et = et_ref[...]
    cq = tot_ref[0] + lf_ref[0]
    s = _dot((kc_ref[0] * q).astype(BF16), e) * scale
    s = s + (cq - ck_ref[0])
    m_prev = m_ref[...]
    m_new = jnp.maximum(m_prev, jnp.max(s, axis=0, keepdims=True))
    alpha = jnp.exp(m_prev - m_new)
    pe = jnp.exp(s - m_new)
    l_new = alpha * l_ref[...] + jnp.sum(pe, axis=0, keepdims=True)
    pv = _dot(pe.astype(BF16), et) * vc_ref[0]
    acc_new = acc_ref[...] * _expand3(alpha, et) + _sum8(pv)
    m_ref[...] = m_new
    l_ref[...] = l_new
    acc_ref[...] = acc_new

    @pl.when(p == npg - 1)
    def _():
        kn = jnp.broadcast_to(kn_ref[0] * q, (16, q.shape[1]))
        s_n = _dot(kn.astype(BF16), e)[0:1] * scale + (cq - cq)
        m_fin = jnp.maximum(m_new, s_n)
        a_fin = jnp.exp(m_new - m_fin)
        p_n = jnp.exp(s_n - m_fin)
        l_fin = a_fin * l_new + p_n
        acc1 = jnp.sum(acc_new, axis=0, keepdims=True)
        out = acc1 * _expand3(a_fin / l_fin, et) + _expand3(p_n / l_fin, et) * vn_ref[0]
        o_ref[0] = out.astype(o_ref.dtype)


def fox_decode(q, k_cache, v_cache, table, ck_past, total, logf_new, k_new, v_new, n_heads):
    b, npg = table.shape
    hd = q.shape[-1]
    head_of = lax.broadcasted_iota(jnp.int32, (hd, n_heads), 0) // HEAD
    e = (head_of == lax.broadcasted_iota(jnp.int32, (hd, n_heads), 1)).astype(BF16)
    et = e.T
    row = lambda w: pl.BlockSpec((1, 1, w), lambda i, p, pt: (i, 0, 0))
    return pl.pallas_call(
        functools.partial(_fox_dec_body, scale=HEAD ** -0.5),
        grid_spec=pltpu.PrefetchScalarGridSpec(
            num_scalar_prefetch=1,
            grid=(b, npg),
            in_specs=[row(hd),
                      pl.BlockSpec((1, PAGE, hd), lambda i, p, pt: (pt[i, p], 0, 0)),
                      pl.BlockSpec((1, PAGE, hd), lambda i, p, pt: (pt[i, p], 0, 0)),
                      pl.BlockSpec((1, PAGE, n_heads), lambda i, p, pt: (i, p, 0)),
                      row(n_heads), row(n_heads), row(hd), row(hd),
                      pl.BlockSpec((hd, n_heads), lambda i, p, pt: (0, 0)),
                      pl.BlockSpec((n_heads, hd), lambda i, p, pt: (0, 0))],
            out_specs=row(hd),
            scratch_shapes=[pltpu.VMEM((1, n_heads), F32), pltpu.VMEM((1, n_heads), F32),
                            pltpu.VMEM((8, hd), F32)]),
        out_shape=jax.ShapeDtypeStruct((b, 1, hd), BF16),
        compiler_params=_cparams(("parallel", "arbitrary")),
        name="fox_decode",
    )(table, q, k_cache, v_cache, ck_past, total, logf_new, k_new, v_new, e, et)


def _prep_weights(w_up, w_down, a_w_in, a_w_out, w_k, w_v, w_f, b_w_q, b_w_o, conv_dim, v_dim):
    n_main = conv_dim + v_dim
    n_ab = a_w_in.shape[-1] - n_main
    return dict(
        w_up=w_up.astype(BF16), w_down=w_down.astype(BF16),
        w_in=a_w_in[:, :, :n_main].astype(BF16),
        w_ab=jnp.pad(a_w_in[:, :, n_main:], ((0, 0), (0, 0), (0, HEAD - n_ab))).astype(BF16),
        w_out=a_w_out.astype(BF16),
        w_kv=jnp.concatenate([w_k, w_v], axis=1).astype(BF16),
        w_f=jnp.pad(w_f, ((0, 0), (0, HEAD - w_f.shape[1]))).astype(BF16),
        w_q=b_w_q.astype(BF16), w_o=b_w_o.astype(BF16))


def _mlp(h, g, w_up, w_down):
    hid = matmul(rmsnorm(h, g), w_up, out_dtypes=(BF16,), relu2=True)
    return matmul(hid, w_down, res=h)


def _trunk(x, conv_state, delta_state, past, wts, prm):
    b, l, d = x.shape
    m = b * l
    n_a = prm["a_log"].shape[0]
    n_h = prm["a_log"].shape[1]
    qk_dim = n_h * HEAD
    conv_dim = prm["a_w_conv"].shape[-1]
    decode = past is not None
    h = x.reshape(m, d)
    new_conv, new_delta = [], []
    for li in range(n_a):
        xn = rmsnorm(h, prm["norm_mix_g"][li])
        proj = matmul(xn, wts["w_in"][li])
        ab = matmul(xn, wts["w_ab"][li], tk=4096)
        gb = gdn_gates(ab, prm["a_log"][li], prm["a_dt_bias"][li])
        g, beta = gb[:, :n_h], gb[:, n_h:2 * n_h]
        if decode:
            buf = conv_state[li]
            qkv = conv_decode(proj, buf, prm["a_w_conv"][li], qk_dim)
            new_conv.append(jnp.concatenate([buf[:, 1:], proj[:, None, :conv_dim]], axis=1))
            hv = lambda t: t.reshape(b, n_h, HEAD)
            rep = lambda t: jnp.broadcast_to(t[:, :, None], (b, n_h, HEAD))
            o, s_new = delta_decode(hv(qkv[:, :qk_dim]), hv(qkv[:, qk_dim:2 * qk_dim]),
                                    hv(qkv[:, 2 * qk_dim:]), hv(proj[:, conv_dim:]),
                                    rep(g), rep(beta), delta_state[li], prm["a_o_norm_g"][li])
            o = o.reshape(m, n_h * HEAD).astype(BF16)
        else:
            proj3 = proj.reshape(b, l, -1)
            buf8 = jnp.pad(conv_state[li], ((0, 0), (8 - (CONV_W - 1), 0), (0, 0)))
            qkv = conv_prompt(proj3, buf8, prm["a_w_conv"][li], qk_dim)
            new_conv.append(proj3[:, l - (CONV_W - 1):, :conv_dim])
            chunks = lambda t: t.reshape(b, l // CHUNK, CHUNK, n_h).transpose(0, 3, 1, 2)
            o, s_new = delta_prompt(qkv, proj3, chunks(g), chunks(beta), delta_state[li],
                                    prm["a_o_norm_g"][li], n_h)
            o = o.reshape(m, n_h * HEAD)
        new_delta.append(s_new)
        h = matmul(o, wts["w_out"][li], res=h)
        h = _mlp(h, prm["norm_mlp_g"][li], wts["w_up"][li], wts["w_down"][li])

    hn = rmsnorm(h, prm["kv_norm_g"])
    nb = wts["w_kv"].shape[1] // 2
    n_hb = nb // HEAD
    kv = matmul(hn, wts["w_kv"])
    k_new, v_new = kv[:, :nb], kv[:, nb:]
    logf = bias_log_sigmoid(matmul(hn, wts["w_f"], tk=4096), prm["b_f"])[:, :n_hb]
    if decode:
        k_cache, v_cache, logf_cache, table = past
        ck_past, total = paged_cumsum(logf_cache, table)
        k_cache = k_cache.reshape(k_cache.shape[0], PAGE, nb)
        v_cache = v_cache.reshape(v_cache.shape[0], PAGE, nb)
    else:
        npg = l // PAGE
        table = jnp.arange(b * npg, dtype=jnp.int32).reshape(b, npg)
        cum, _ = paged_cumsum(logf.reshape(b * npg, PAGE, n_hb), table)
        cum_t = cum.transpose(0, 2, 1).reshape(b, n_hb, 1, l)
        k_bf = k_new.astype(BF16).reshape(b, l, nb)
        v_bf = v_new.astype(BF16).reshape(b, l, nb)
    for j in range(prm["b_w_q"].shape[0]):
        li = n_a + j
        xn = rmsnorm(h, prm["norm_mix_g"][li])
        if decode:
            q = matmul(xn, wts["w_q"][j])
            o = fox_decode(q.reshape(b, 1, nb), k_cache, v_cache, table, ck_past, total,
                           logf.reshape(b, 1, n_hb), k_new.reshape(b, 1, nb),
                           v_new.reshape(b, 1, nb), n_hb)
        else:
            q = matmul(xn, wts["w_q"][j], out_dtypes=(BF16,))
            o = fox_prompt(q.reshape(b, l, nb), k_bf, v_bf, cum, cum_t, n_hb)
        h = matmul(o.reshape(m, nb), wts["w_o"][j], res=h)
        h = _mlp(h, prm["norm_mlp_g"][li], wts["w_up"][li], wts["w_down"][li])
    y = rmsnorm(h, prm["final_norm_g"], out_dtype=F32)
    return (y.reshape(b, l, d), jnp.stack(new_delta), jnp.stack(new_conv),
            k_new.reshape(b, l, n_hb, HEAD), v_new.reshape(b, l, n_hb, HEAD),
            logf.reshape(b, l, n_hb))


def kernel(x_prompt, x_sample, cache_k, cache_v, cache_logf, state_delta, state_conv, page_table,
           norm_mix_g, norm_mlp_g, w_up, w_down, a_w_in, a_w_conv, a_log, a_dt_bias, a_o_norm_g,
           a_w_out, kv_norm_g, w_k, w_v, w_f, b_f, b_w_q, b_w_o, final_norm_g):
    conv_dim = a_w_conv.shape[-1]
    v_dim = a_w_out.shape[1]
    wts = _prep_weights(w_up, w_down, a_w_in, a_w_out, w_k, w_v, w_f, b_w_q, b_w_o, conv_dim, v_dim)
    prm = dict(norm_mix_g=norm_mix_g, norm_mlp_g=norm_mlp_g, a_w_conv=a_w_conv, a_log=a_log,
               a_dt_bias=a_dt_bias, a_o_norm_g=a_o_norm_g, kv_norm_g=kv_norm_g, b_f=b_f,
               b_w_q=b_w_q, final_norm_g=final_norm_g)
    n_a = a_log.shape[0]
    n_p = x_prompt.shape[0]
    n_h = a_log.shape[1]
    zero_conv = jnp.zeros((n_a, n_p, CONV_W - 1, conv_dim), state_conv.dtype)
    zero_delta = jnp.zeros((n_a, n_p, n_h, HEAD, HEAD), state_delta.dtype)
    y_p, delta_p, conv_p, k_p, v_p, logf_p = _trunk(x_prompt, zero_conv, zero_delta, None, wts, prm)
    y_s, delta_s, conv_s, k_s, v_s, logf_s = _trunk(
        x_sample, state_conv, state_delta, (cache_k, cache_v, cache_logf, page_table), wts, prm)
    return (y_p, y_s, delta_p, conv_p, k_p, v_p, logf_p, delta_s, conv_s, k_s, v_s, logf_s)
```

```python
import functools

import jax
import jax.numpy as jnp
from jax import lax
from jax.experimental import pallas as pl
from jax.experimental.pallas import tpu as pltpu

F32 = jnp.float32
BF16 = jnp.bfloat16

EPS = 1e-6
HEAD = 128
CHUNK = 64
CONV_W = 4
PAGE = 128
NEG = -1e30
V7X_VMEM_LIMIT = 56 * 1024 * 1024


def _cparams(sem):
    return pltpu.CompilerParams(dimension_semantics=sem, vmem_limit_bytes=V7X_VMEM_LIMIT)


def _rms_body(x_ref, g_ref, o_ref):
    x = x_ref[...]
    ms = jnp.mean(x * x, axis=-1, keepdims=True)
    o_ref[...] = (x * lax.rsqrt(ms + EPS) * g_ref[...]).astype(o_ref.dtype)


def rmsnorm(x, g, out_dtype=BF16):
    m, d = x.shape
    tm = min(m, 256)
    return pl.pallas_call(
        _rms_body,
        grid=(m // tm,),
        in_specs=[pl.BlockSpec((tm, d), lambda i: (i, 0)),
                  pl.BlockSpec((1, d), lambda i: (0, 0))],
        out_specs=pl.BlockSpec((tm, d), lambda i: (i, 0)),
        out_shape=jax.ShapeDtypeStruct((m, d), out_dtype),
        compiler_params=_cparams(("parallel",)),
        name="rmsnorm",
    )(x, g.reshape(1, d))


def _mm_body(*refs, nk, relu2, has_res, n_out):
    a_ref, w_ref = refs[0], refs[1]
    pos = 2
    r_ref = None
    if has_res:
        r_ref = refs[pos]
        pos += 1
    o_refs = refs[pos:pos + n_out]
    scr = refs[pos + n_out:]

    def finish(acc):
        if relu2:
            acc = jnp.maximum(acc, 0.0)
            acc = acc * acc
        if has_res:
            acc = r_ref[...] + acc
        for o_ref in o_refs:
            o_ref[...] = acc.astype(o_ref.dtype)

    part = jnp.dot(a_ref[...], w_ref[...], preferred_element_type=F32)
    if nk == 1:
        finish(part)
    else:
        acc_ref = scr[0]
        k = pl.program_id(2)

        @pl.when(k == 0)
        def _():
            acc_ref[...] = part

        @pl.when(jnp.logical_and(k > 0, k < nk - 1))
        def _():
            acc_ref[...] += part

        @pl.when(k == nk - 1)
        def _():
            finish(acc_ref[...] + part)


def matmul(a, w, out_dtypes=(F32,), res=None, relu2=False, tm=1024, tn=1024, tk=4096):
    m, kd = a.shape
    n = w.shape[1]
    if kd > tk:
        tk //= 2
    tm, tn, tk = min(tm, m), min(tn, n), min(tk, kd)
    nk = kd // tk
    in_specs = [pl.BlockSpec((tm, tk), lambda i, j, k: (i, k)),
                pl.BlockSpec((tk, tn), lambda i, j, k: (k, j))]
    args = [a, w]
    if res is not None:
        in_specs.append(pl.BlockSpec((tm, tn), lambda i, j, k: (i, j)))
        args.append(res)
    out = pl.pallas_call(
        functools.partial(_mm_body, nk=nk, relu2=relu2, has_res=res is not None,
                          n_out=len(out_dtypes)),
        grid=(m // tm, n // tn, nk),
        in_specs=in_specs,
        out_specs=[pl.BlockSpec((tm, tn), lambda i, j, k: (i, j)) for _ in out_dtypes],
        out_shape=[jax.ShapeDtypeStruct((m, n), dt) for dt in out_dtypes],
        scratch_shapes=[pltpu.VMEM((tm, tn), F32)] if nk > 1 else [],
        compiler_params=_cparams(("parallel", "parallel", "arbitrary")),
        name="matmul",
    )(*args)
    return out[0] if len(out_dtypes) == 1 else tuple(out)


def _softplus(x):
    return jnp.maximum(x, 0.0) + jnp.log1p(jnp.exp(-jnp.abs(x)))


def _gates_body(x_ref, alog_ref, dt_ref, o_ref, *, nh):
    x = x_ref[...]
    lane = lax.broadcasted_iota(jnp.int32, x.shape, 1)
    g = -jnp.exp(alog_ref[...]) * _softplus(x + dt_ref[...])
    beta = jax.nn.sigmoid(x)
    o_ref[...] = jnp.where(lane < nh, g, beta)


def gdn_gates(ab, a_log, dt_bias):
    m, w = ab.shape
    nh = a_log.shape[0]
    tm = min(m, 1024)
    pad = lambda v: jnp.pad(v.astype(F32), (0, w - nh)).reshape(1, w)
    return pl.pallas_call(
        functools.partial(_gates_body, nh=nh),
        grid=(m // tm,),
        in_specs=[pl.BlockSpec((tm, w), lambda i: (i, 0)),
                  pl.BlockSpec((1, w), lambda i: (0, 0)),
                  pl.BlockSpec((1, w), lambda i: (0, 0))],
        out_specs=pl.BlockSpec((tm, w), lambda i: (i, 0)),
        out_shape=jax.ShapeDtypeStruct((m, w), F32),
        compiler_params=_cparams(("parallel",)),
        name="gdn_gates",
    )(ab, pad(a_log), pad(dt_bias))


def _logsig_body(x_ref, b_ref, o_ref):
    y = x_ref[...] + b_ref[...]
    o_ref[...] = -_softplus(-y)


def bias_log_sigmoid(x, b):
    m, w = x.shape
    tm = min(m, 1024)
    return pl.pallas_call(
        _logsig_body,
        grid=(m // tm,),
        in_specs=[pl.BlockSpec((tm, w), lambda i: (i, 0)),
                  pl.BlockSpec((1, w), lambda i: (0, 0))],
        out_specs=pl.BlockSpec((tm, w), lambda i: (i, 0)),
        out_shape=jax.ShapeDtypeStruct((m, w), F32),
        compiler_params=_cparams(("parallel",)),
        name="bias_log_sigmoid",
    )(x, jnp.pad(b.astype(F32), (0, w - b.shape[0])).reshape(1, w))


def _l2_heads(y, o_ref, idx, qscale):
    for j in range(y.shape[-1] // HEAD):
        ys = y[..., j * HEAD:(j + 1) * HEAD]
        ss = jnp.sum(ys * ys, axis=-1, keepdims=True)
        o_ref[idx + (slice(j * HEAD, (j + 1) * HEAD),)] = ys * lax.rsqrt(ss + EPS) * qscale


def _conv_body(x_ref, buf_ref, w_ref, o_ref, ext_ref, *, tl, n_q_tiles):
    c = pl.program_id(1)
    t = pl.program_id(2)

    @pl.when(t == 0)
    def _():
        ext_ref[0:8, :] = buf_ref[0]

    u = x_ref[0]
    ext_ref[8:8 + tl, :] = u
    w = w_ref[...]
    acc = ext_ref[5:5 + tl, :] * w[0:1]
    acc = acc + ext_ref[6:6 + tl, :] * w[1:2]
    acc = acc + ext_ref[7:7 + tl, :] * w[2:3]
    acc = acc + u * w[3:4]
    y = acc * jax.nn.sigmoid(acc)

    @pl.when(c < 2 * n_q_tiles)
    def _():
        qscale = jnp.where(c < n_q_tiles, HEAD ** -0.5, 1.0).astype(F32)
        _l2_heads(y, o_ref, (0, slice(None)), qscale)

    @pl.when(c >= 2 * n_q_tiles)
    def _():
        o_ref[0] = y

    ext_ref[0:8, :] = ext_ref[tl:tl + 8, :]


def conv_prompt(proj, buf8, w_conv, qk_dim, tl=512, tc=512):
    b, l, _ = proj.shape
    cd = w_conv.shape[1]
    tl, tc = min(tl, l), min(tc, qk_dim)
    return pl.pallas_call(
        functools.partial(_conv_body, tl=tl, n_q_tiles=qk_dim // tc),
        grid=(b, cd // tc, l // tl),
        in_specs=[pl.BlockSpec((1, tl, tc), lambda i, c, t: (i, t, c)),
                  pl.BlockSpec((1, 8, tc), lambda i, c, t: (i, 0, c)),
                  pl.BlockSpec((CONV_W, tc), lambda i, c, t: (0, c))],
        out_specs=pl.BlockSpec((1, tl, tc), lambda i, c, t: (i, t, c)),
        out_shape=jax.ShapeDtypeStruct((b, l, cd), F32),
        scratch_shapes=[pltpu.VMEM((tl + 8, tc), F32)],
        compiler_params=_cparams(("parallel", "parallel", "arbitrary")),
        name="conv_prompt",
    )(proj, buf8, w_conv)


def _conv_dec_body(u_ref, buf_ref, w_ref, o_ref, *, n_q_tiles):
    c = pl.program_id(0)
    w = w_ref[...]
    acc = buf_ref[:, 0, :] * w[0:1]
    acc = acc + buf_ref[:, 1, :] * w[1:2]
    acc = acc + buf_ref[:, 2, :] * w[2:3]
    acc = acc + u_ref[...] * w[3:4]
    y = acc * jax.nn.sigmoid(acc)

    @pl.when(c < 2 * n_q_tiles)
    def _():
        qscale = jnp.where(c < n_q_tiles, HEAD ** -0.5, 1.0).astype(F32)
        _l2_heads(y, o_ref, (slice(None),), qscale)

    @pl.when(c >= 2 * n_q_tiles)
    def _():
        o_ref[...] = y


def conv_decode(proj, buf, w_conv, qk_dim, tc=512):
    b = proj.shape[0]
    cd = w_conv.shape[1]
    tc = min(tc, qk_dim)
    return pl.pallas_call(
        functools.partial(_conv_dec_body, n_q_tiles=qk_dim // tc),
        grid=(cd // tc,),
        in_specs=[pl.BlockSpec((b, tc), lambda c: (0, c)),
                  pl.BlockSpec((b, CONV_W - 1, tc), lambda c: (0, 0, c)),
                  pl.BlockSpec((CONV_W, tc), lambda c: (0, c))],
        out_specs=pl.BlockSpec((b, tc), lambda c: (0, c)),
        out_shape=jax.ShapeDtypeStruct((b, cd), F32),
        compiler_params=_cparams(("parallel",)),
        name="conv_decode",
    )(proj, buf, w_conv)


def _split(x):
    hi = x.astype(BF16)
    lo = (x - hi.astype(F32)).astype(BF16)
    return hi, lo


def _dot(a, b, dims=(((1,), (0,)), ((), ()))):
    return lax.dot_general(a, b, dims, preferred_element_type=F32)


_NT = (((1,), (1,)), ((), ()))
_TN = (((0,), (0,)), ((), ()))


def _dot1(a, b, dims=(((1,), (0,)), ((), ()))):
    return _dot(a.astype(BF16), b.astype(BF16), dims)


def _dot3(a, b, dims=(((1,), (0,)), ((), ()))):
    ah, al = _split(a)
    bh, bl = _split(b)
    return _dot(ah, bh, dims) + (_dot(ah, bl, dims) + _dot(al, bh, dims))


_BMM = (((2,), (1,)), ((0,), (0,)))
_BMM_NT = (((2,), (2,)), ((0,), (0,)))
_BMM_TN = (((1,), (1,)), ((0,), (0,)))
INV_PASSES = 1


def _bdot(a, b, dims=_BMM, passes=1):
    if passes == 1:
        return _dot(a.astype(BF16), b.astype(BF16), dims)
    ah, al = _split(a)
    bh, bl = _split(b)
    return _dot(ah, bh, dims) + (_dot(ah, bl, dims) + _dot(al, bh, dims))


def _split3(x):
    hi = x.astype(BF16)
    r1 = x - hi.astype(F32)
    mid = r1.astype(BF16)
    lo = (r1 - mid.astype(F32)).astype(BF16)
    return hi, mid, lo


def _unit_lower_inverse(a_strict, ii, jj):
    c = a_strict.shape[-1]
    eye = (ii == jj).astype(F32)
    m = None
    s = 1
    k = 0
    while s < c:
        off = jnp.logical_and((ii >> (k + 1)) == (jj >> (k + 1)),
                              jnp.logical_and(((ii >> k) & 1) == 1, ((jj >> k) & 1) == 0))
        a_off = jnp.where(off, a_strict, 0.0)
        if m is None:
            m = eye - a_off
        else:
            m = m - _bdot(m, _bdot(a_off, m, passes=INV_PASSES), passes=INV_PASSES)
        s *= 2
        k += 1
    return m


def _delta_body(q_ref, k_ref, v_ref, z_ref, g_ref, b_ref, s0_ref, gn_ref, o_ref, sf_ref, s_ref,
                *, hb, tl):
    t = pl.program_id(2)
    nt = pl.num_programs(2)
    c = CHUNK
    npc = tl // c

    @pl.when(t == 0)
    def _():
        s_ref[...] = s0_ref[0]

    ii = lax.broadcasted_iota(jnp.int32, (c, c), 0)
    jj = lax.broadcasted_iota(jnp.int32, (c, c), 1)
    tril = ii >= jj
    stril = ii > jj
    tril_b = tril.astype(BF16)
    triu_b = (ii <= jj).astype(BF16)
    eye_b = (ii == jj).astype(BF16)
    gn = gn_ref[...]

    row0 = pl.multiple_of(t * npc, npc)
    g_rows = jnp.concatenate([g_ref[0, hh, pl.ds(row0, npc), :] for hh in range(hb)], axis=0)
    b_rows = jnp.concatenate([b_ref[0, hh, pl.ds(row0, npc), :] for hh in range(hb)], axis=0)
    g3 = _split3(g_rows)
    b3 = _split3(b_rows)
    dcum_rows = _dot(g3[0], triu_b) + (_dot(g3[1], triu_b) + _dot(g3[2], triu_b))
    dcum_cols = _dot(tril_b, g3[0], _NT) + (_dot(tril_b, g3[1], _NT) + _dot(tril_b, g3[2], _NT))
    b_cols = _dot(eye_b, b3[0], _NT) + (_dot(eye_b, b3[1], _NT) + _dot(eye_b, b3[2], _NT))

    ks, qs, decs, rhss, dcs, bcs = [], [], [], [], [], []
    for j in range(npc):
        rs = slice(j * c, (j + 1) * c)
        for hh in range(hb):
            hs = slice(hh * HEAD, (hh + 1) * HEAD)
            gi = hh * npc + j
            k = k_ref[0, rs, hs]
            d_col = dcum_cols[:, gi:gi + 1]
            b_col = b_cols[:, gi:gi + 1]
            diff = d_col - dcum_rows[gi:gi + 1, :]
            decs.append(jnp.where(tril, jnp.exp(jnp.where(tril, diff, 0.0)), 0.0))
            rhss.append(jnp.concatenate([v_ref[0, rs, hs] * b_col,
                                         k * (b_col * jnp.exp(d_col))], axis=1))
            ks.append(k)
            qs.append(q_ref[0, rs, hs])
            dcs.append(d_col)
            bcs.append(b_col)
    kb = jnp.stack(ks)
    qb = jnp.stack(qs)
    dec = jnp.stack(decs)
    d_colb = jnp.stack(dcs)
    d_lastb = d_colb[:, c - 1:c, :]
    a_mat = jnp.stack(bcs) * _bdot(kb, kb, _BMM_NT, passes=INV_PASSES) * jnp.where(stril, dec, 0.0)
    sol = _bdot(_unit_lower_inverse(a_mat, ii, jj), jnp.stack(rhss), passes=INV_PASSES)
    attn = _bdot(qb, kb, _BMM_NT) * dec
    wq = jnp.concatenate([sol[:, :, HEAD:], qb * jnp.exp(d_colb)], axis=1)
    k_dec = kb * jnp.exp(d_lastb - d_colb)
    g_last = jnp.exp(d_lastb)

    s_mat = s_ref[...]
    for j in range(npc):
        ps = slice(j * hb, (j + 1) * hb)
        rs = slice(j * c, (j + 1) * c)
        wq_s = _bdot(wq[ps], s_mat)
        v_new = sol[ps, :, :HEAD] - wq_s[:, :c]
        o = wq_s[:, c:] + _bdot(attn[ps], v_new)
        s_mat = s_mat * g_last[ps] + _bdot(k_dec[ps], v_new, _BMM_TN)
        ms = jnp.mean(o * o, axis=-1, keepdims=True)
        on = o * lax.rsqrt(ms + EPS) * gn
        for hh in range(hb):
            hs = slice(hh * HEAD, (hh + 1) * HEAD)
            z = z_ref[0, rs, hs]
            o_ref[0, rs, hs] = (on[hh] * (z * jax.nn.sigmoid(z))).astype(o_ref.dtype)
    s_ref[...] = s_mat

    @pl.when(t == nt - 1)
    def _():
        sf_ref[0] = s_ref[...]


def delta_prompt(qkv, proj, g, beta, s0, gnorm, n_heads, hb=4, tl=512):
    b, l, cd = qkv.shape
    tl, hb = min(tl, l), min(hb, n_heads)
    nhg = n_heads // hb
    wb = hb * HEAD
    nc = l // CHUNK
    z_off = cd // wb
    return pl.pallas_call(
        functools.partial(_delta_body, hb=hb, tl=tl),
        grid=(b, nhg, l // tl),
        in_specs=[pl.BlockSpec((1, tl, wb), lambda i, h, t: (i, t, h)),
                  pl.BlockSpec((1, tl, wb), lambda i, h, t: (i, t, nhg + h)),
                  pl.BlockSpec((1, tl, wb), lambda i, h, t: (i, t, 2 * nhg + h)),
                  pl.BlockSpec((1, tl, wb), lambda i, h, t: (i, t, z_off + h)),
                  pl.BlockSpec((1, hb, nc, CHUNK), lambda i, h, t: (i, h, 0, 0)),
                  pl.BlockSpec((1, hb, nc, CHUNK), lambda i, h, t: (i, h, 0, 0)),
                  pl.BlockSpec((1, hb, HEAD, HEAD), lambda i, h, t: (i, h, 0, 0)),
                  pl.BlockSpec((1, HEAD), lambda i, h, t: (0, 0))],
        out_specs=[pl.BlockSpec((1, tl, wb), lambda i, h, t: (i, t, h)),
                   pl.BlockSpec((1, hb, HEAD, HEAD), lambda i, h, t: (i, h, 0, 0))],
        out_shape=[jax.ShapeDtypeStruct((b, l, n_heads * HEAD), BF16),
                   jax.ShapeDtypeStruct((b, n_heads, HEAD, HEAD), F32)],
        scratch_shapes=[pltpu.VMEM((hb, HEAD, HEAD), F32)],
        compiler_params=_cparams(("parallel", "parallel", "arbitrary")),
        name="delta_prompt",
    )(qkv, qkv, qkv, proj, g, beta, s0, gnorm.reshape(1, HEAD))


def _delta_dec_body(q_ref, k_ref, v_ref, z_ref, g_ref, b_ref, s0_ref, gn_ref, o_ref, sf_ref, *, nh):
    ii = lax.broadcasted_iota(jnp.int32, (HEAD, HEAD), 0)
    jj = lax.broadcasted_iota(jnp.int32, (HEAD, HEAD), 1)
    eye = ii == jj
    gn = gn_ref[...]

    def col_of(row):
        return jnp.sum(jnp.where(eye, jnp.broadcast_to(row, (HEAD, HEAD)), 0.0), axis=1, keepdims=True)

    def head(h, carry):
        q = q_ref[0, pl.ds(h, 1), :]
        k = k_ref[0, pl.ds(h, 1), :]
        v = v_ref[0, pl.ds(h, 1), :]
        z = z_ref[0, pl.ds(h, 1), :]
        eg = jnp.exp(g_ref[0, pl.ds(h, 1), :])
        beta = b_ref[0, pl.ds(h, 1), :]
        s_mat = s0_ref[0, h]
        k_col = col_of(k)
        ks = jnp.sum(k_col * s_mat, axis=0, keepdims=True)
        qs = jnp.sum(col_of(q) * s_mat, axis=0, keepdims=True)
        v_new = v * beta - (beta * eg) * ks
        qk = jnp.sum(q * k, axis=-1, keepdims=True)
        o = eg * qs + qk * v_new
        sf_ref[0, h] = s_mat * eg + k_col * v_new
        ms = jnp.mean(o * o, axis=-1, keepdims=True)
        on = o * lax.rsqrt(ms + EPS) * gn
        o_ref[0, pl.ds(h, 1), :] = (on * (z * jax.nn.sigmoid(z))).astype(o_ref.dtype)
        return carry

    lax.fori_loop(0, nh, head, 0)


def delta_decode(q, k, v, z, g_rep, b_rep, s0, gnorm):
    b, nh, _ = q.shape
    vec = pl.BlockSpec((1, nh, HEAD), lambda i: (i, 0, 0))
    st = pl.BlockSpec((1, nh, HEAD, HEAD), lambda i: (i, 0, 0, 0))
    return pl.pallas_call(
        functools.partial(_delta_dec_body, nh=nh),
        grid=(b,),
        in_specs=[vec, vec, vec, vec, vec, vec, st, pl.BlockSpec((1, HEAD), lambda i: (0, 0))],
        out_specs=[vec, st],
        out_shape=[jax.ShapeDtypeStruct((b, nh, HEAD), F32),
                   jax.ShapeDtypeStruct((b, nh, HEAD, HEAD), F32)],
        compiler_params=_cparams(("parallel",)),
        name="delta_decode",
    )(q, k, v, z, g_rep, b_rep, s0, gnorm.reshape(1, HEAD))


def _cumsum_body(pt_ref, x_ref, o_ref, tot_ref, carry_ref):
    del pt_ref
    p = pl.program_id(1)
    npg = pl.num_programs(1)

    @pl.when(p == 0)
    def _():
        carry_ref[...] = jnp.zeros_like(carry_ref)

    x = x_ref[0]
    n = x.shape[0]
    tri = (lax.broadcasted_iota(jnp.int32, (n, n), 0)
           >= lax.broadcasted_iota(jnp.int32, (n, n), 1)).astype(BF16)
    hi = x.astype(BF16)
    r1 = x - hi.astype(F32)
    mid = r1.astype(BF16)
    lo = (r1 - mid.astype(F32)).astype(BF16)
    cs = _dot(tri, hi) + (_dot(tri, mid) + _dot(tri, lo))
    out = cs + carry_ref[...]
    o_ref[0] = out
    carry_ref[...] = out[n - 1:n, :]

    @pl.when(p == npg - 1)
    def _():
        tot_ref[0] = out[n - 1:n, :]


def paged_cumsum(pages, table):
    b, npg = table.shape
    _, pg, h = pages.shape
    return pl.pallas_call(
        _cumsum_body,
        grid_spec=pltpu.PrefetchScalarGridSpec(
            num_scalar_prefetch=1,
            grid=(b, npg),
            in_specs=[pl.BlockSpec((1, pg, h), lambda i, p, pt: (pt[i, p], 0, 0))],
            out_specs=[pl.BlockSpec((1, pg, h), lambda i, p, pt: (i, p, 0)),
                       pl.BlockSpec((1, 1, h), lambda i, p, pt: (i, 0, 0))],
            scratch_shapes=[pltpu.VMEM((1, h), F32)]),
        out_shape=[jax.ShapeDtypeStruct((b, npg * pg, h), F32),
                   jax.ShapeDtypeStruct((b, 1, h), F32)],
        compiler_params=_cparams(("parallel", "arbitrary")),
        name="paged_cumsum",
    )(table, pages)


def _flash_body(q_ref, k_ref, v_ref, cum_ref, ckt_ref, o_ref, m_ref, l_ref, acc_ref, cq_ref,
                *, tq, tk, scale):
    h = pl.program_id(1)
    qi = pl.program_id(2)
    ki = pl.program_id(3)
    nk = pl.num_programs(3)

    @pl.when(ki == 0)
    def _():
        m_ref[...] = jnp.full_like(m_ref, NEG)
        l_ref[...] = jnp.zeros_like(l_ref)
        acc_ref[...] = jnp.zeros_like(acc_ref)
        cum = cum_ref[0]
        lane = lax.broadcasted_iota(jnp.int32, cum.shape, 1)
        cq_ref[...] = jnp.sum(jnp.where(lane == h, cum, 0.0), axis=1, keepdims=True)

    @pl.when(ki * tk <= qi * tq + (tq - 1))
    def _():
        s = _dot(q_ref[0], k_ref[0], _NT) * scale
        s = s + cq_ref[...] - ckt_ref[0, 0]
        qpos = qi * tq + lax.broadcasted_iota(jnp.int32, (tq, tk), 0)
        kpos = ki * tk + lax.broadcasted_iota(jnp.int32, (tq, tk), 1)
        s = jnp.where(qpos >= kpos, s, NEG)
        m_prev = m_ref[...]
        m_new = jnp.maximum(m_prev, jnp.max(s, axis=1, keepdims=True))
        alpha = jnp.exp(m_prev - m_new)
        p = jnp.exp(s - m_new)
        l_ref[...] = alpha * l_ref[...] + jnp.sum(p, axis=1, keepdims=True)
        acc_ref[...] = alpha * acc_ref[...] + _dot(p.astype(BF16), v_ref[0])
        m_ref[...] = m_new

    @pl.when(ki == nk - 1)
    def _():
        o_ref[0] = (acc_ref[...] / l_ref[...]).astype(o_ref.dtype)


def fox_prompt(q, k, v, cum, cum_t, n_heads, tq=1024, tk=1024):
    b, l, _ = q.shape
    tq, tk = min(tq, l), min(tk, l)
    kv_map = lambda i, h, qi, ki: (i, jnp.minimum(ki, (qi * tq + tq - 1) // tk), h)
    return pl.pallas_call(
        functools.partial(_flash_body, tq=tq, tk=tk, scale=HEAD ** -0.5),
        grid=(b, n_heads, l // tq, l // tk),
        in_specs=[pl.BlockSpec((1, tq, HEAD), lambda i, h, qi, ki: (i, qi, h)),
                  pl.BlockSpec((1, tk, HEAD), kv_map),
                  pl.BlockSpec((1, tk, HEAD), kv_map),
                  pl.BlockSpec((1, tq, n_heads), lambda i, h, qi, ki: (i, qi, 0)),
                  pl.BlockSpec((1, 1, 1, tk),
                               lambda i, h, qi, ki: (i, h, 0, jnp.minimum(ki, (qi * tq + tq - 1) // tk)))],
        out_specs=pl.BlockSpec((1, tq, HEAD), lambda i, h, qi, ki: (i, qi, h)),
        out_shape=jax.ShapeDtypeStruct(q.shape, BF16),
        scratch_shapes=[pltpu.VMEM((tq, 1), F32), pltpu.VMEM((tq, 1), F32),
                        pltpu.VMEM((tq, HEAD), F32), pltpu.VMEM((tq, 1), F32)],
        compiler_params=_cparams(("parallel", "parallel", "parallel", "arbitrary")),
        name="fox_prompt",
    )(q, k, v, cum, cum_t)


def _lane_pack3(x, nh):
    pieces = _split3(x)
    row = lax.broadcasted_iota(jnp.int32, (nh, HEAD), 0)
    lane = lax.broadcasted_iota(jnp.int32, (nh, HEAD), 1)
    out = None
    for gi, pc in enumerate(pieces):
        part = _dot(pc, (lane == gi * nh + row).astype(BF16))
        out = part if out is None else out + part
    return out


def _fox_dec_body(pt_ref, q_ref, kc_ref, vc_ref, ck_ref, tot_ref, lf_ref, kn_ref, vn_ref,
                  o_ref, m_ref, l_ref, acc_ref, *, scale, nh):
    del pt_ref
    p = pl.program_id(1)
    npg = pl.num_programs(1)

    @pl.when(p == 0)
    def _():
        m_ref[...] = jnp.full_like(m_ref, NEG)
        l_ref[...] = jnp.zeros_like(l_ref)
        acc_ref[...] = jnp.zeros_like(acc_ref)

    q = q_ref[0]
    ones = jnp.ones((HEAD, HEAD), BF16)
    sub = lax.broadcasted_iota(jnp.int32, (nh, HEAD), 0)
    lane = lax.broadcasted_iota(jnp.int32, (nh, HEAD), 1)
    own = jnp.logical_and(lax.rem(lane, nh) == sub, lane < 3 * nh)

    def head_rows(packed):
        r = packed.shape[0]
        z = jnp.where(own[None], jnp.broadcast_to(packed[:, None, :], (r, nh, HEAD)), 0.0)
        return _dot(z.reshape(r * nh, HEAD).astype(BF16), ones).reshape(r, nh, HEAD)

    cq = head_rows(_lane_pack3(jnp.broadcast_to(tot_ref[0] + lf_ref[0], (8, nh)), nh))[0]
    ck = head_rows(_lane_pack3(ck_ref[0], nh))
    kc = kc_ref[...].reshape(PAGE, nh, HEAD)
    vc = vc_ref[...].reshape(PAGE, nh, HEAD)
    qk = _dot((kc * q[None]).reshape(PAGE * nh, HEAD).astype(BF16), ones).reshape(PAGE, nh, HEAD)
    s = qk * scale + (cq[None] - ck)
    m_prev = m_ref[...]
    m_new = jnp.maximum(m_prev, jnp.max(s, axis=0))
    alpha = jnp.exp(m_prev - m_new)
    pe = jnp.exp(s - m_new[None])
    l_new = alpha * l_ref[...] + jnp.sum(pe, axis=0)
    acc_new = alpha * acc_ref[...] + jnp.sum(pe * vc, axis=0)
    m_ref[...] = m_new
    l_ref[...] = l_new
    acc_ref[...] = acc_new

    @pl.when(p == npg - 1)
    def _():
        s_n = jnp.sum(kn_ref[0] * q, axis=-1, keepdims=True) * scale + (cq - cq)
        m_fin = jnp.maximum(m_new, s_n)
        a_fin = jnp.exp(m_new - m_fin)
        p_n = jnp.exp(s_n - m_fin)
        l_fin = a_fin * l_new + p_n
        o_ref[0] = (a_fin * acc_new + p_n * vn_ref[0]) / l_fin


def fox_decode(q, k_cache, v_cache, table, ck_past, total, logf_new, k_new, v_new, n_heads):
    b, npg = table.shape
    rows = PAGE * n_heads
    vec = pl.BlockSpec((1, n_heads, HEAD), lambda i, p, pt: (i, 0, 0))
    row = pl.BlockSpec((1, 1, n_heads), lambda i, p, pt: (i, 0, 0))
    page = pl.BlockSpec((rows, HEAD), lambda i, p, pt: (pt[i, p], 0))
    return pl.pallas_call(
        functools.partial(_fox_dec_body, scale=HEAD ** -0.5, nh=n_heads),
        grid_spec=pltpu.PrefetchScalarGridSpec(
            num_scalar_prefetch=1,
            grid=(b, npg),
            in_specs=[vec, page, page,
                      pl.BlockSpec((1, PAGE, n_heads), lambda i, p, pt: (i, p, 0)),
                      row, row, vec, vec],
            out_specs=vec,
            scratch_shapes=[pltpu.VMEM((n_heads, HEAD), F32), pltpu.VMEM((n_heads, HEAD), F32),
                            pltpu.VMEM((n_heads, HEAD), F32)]),
        out_shape=jax.ShapeDtypeStruct((b, n_heads, HEAD), F32),
        compiler_params=_cparams(("parallel", "arbitrary")),
        name="fox_decode",
    )(table, q, k_cache, v_cache, ck_past, total, logf_new, k_new, v_new)


def _prep_weights(w_up, w_down, a_w_in, a_w_out, w_k, w_v, w_f, b_w_q, b_w_o, conv_dim, v_dim):
    n_main = conv_dim + v_dim
    n_ab = a_w_in.shape[-1] - n_main
    return dict(
        w_up=w_up.astype(BF16), w_down=w_down.astype(BF16),
        w_in=a_w_in[:, :, :n_main].astype(BF16),
        w_ab=jnp.pad(a_w_in[:, :, n_main:], ((0, 0), (0, 0), (0, HEAD - n_ab))).astype(BF16),
        w_out=a_w_out.astype(BF16),
        w_k=w_k.astype(BF16), w_v=w_v.astype(BF16),
        w_f=jnp.pad(w_f, ((0, 0), (0, HEAD - w_f.shape[1]))).astype(BF16),
        w_q=b_w_q.astype(BF16), w_o=b_w_o.astype(BF16))


def _mlp(h, g, w_up, w_down):
    hid = matmul(rmsnorm(h, g), w_up, out_dtypes=(BF16,), relu2=True)
    return matmul(hid, w_down, res=h)


def _trunk(x, conv_state, delta_state, past, wts, prm):
    b, l, d = x.shape
    m = b * l
    n_a = prm["a_log"].shape[0]
    n_h = prm["a_log"].shape[1]
    qk_dim = n_h * HEAD
    conv_dim = prm["a_w_conv"].shape[-1]
    decode = past is not None
    h = x.reshape(m, d)
    new_conv, new_delta = [], []
    for li in range(n_a):
        xn = rmsnorm(h, prm["norm_mix_g"][li])
        proj = matmul(xn, wts["w_in"][li])
        ab = matmul(xn, wts["w_ab"][li], tk=4096)
        gb = gdn_gates(ab, prm["a_log"][li], prm["a_dt_bias"][li])
        g, beta = gb[:, :n_h], gb[:, n_h:2 * n_h]
        if decode:
            buf = conv_state[li]
            qkv = conv_decode(proj, buf, prm["a_w_conv"][li], qk_dim)
            new_conv.append(jnp.concatenate([buf[:, 1:], proj[:, None, :conv_dim]], axis=1))
            hv = lambda t: t.reshape(b, n_h, HEAD)
            rep = lambda t: jnp.broadcast_to(t[:, :, None], (b, n_h, HEAD))
            o, s_new = delta_decode(hv(qkv[:, :qk_dim]), hv(qkv[:, qk_dim:2 * qk_dim]),
                                    hv(qkv[:, 2 * qk_dim:]), hv(proj[:, conv_dim:]),
                                    rep(g), rep(beta), delta_state[li], prm["a_o_norm_g"][li])
            o = o.reshape(m, n_h * HEAD).astype(BF16)
        else:
            proj3 = proj.reshape(b, l, -1)
            buf8 = jnp.pad(conv_state[li], ((0, 0), (8 - (CONV_W - 1), 0), (0, 0)))
            qkv = conv_prompt(proj3, buf8, prm["a_w_conv"][li], qk_dim)
            new_conv.append(proj3[:, l - (CONV_W - 1):, :conv_dim])
            chunks = lambda t: t.reshape(b, l // CHUNK, CHUNK, n_h).transpose(0, 3, 1, 2)
            o, s_new = delta_prompt(qkv, proj3, chunks(g), chunks(beta), delta_state[li],
                                    prm["a_o_norm_g"][li], n_h)
            o = o.reshape(m, n_h * HEAD)
        new_delta.append(s_new)
        h = matmul(o, wts["w_out"][li], res=h)
        h = _mlp(h, prm["norm_mlp_g"][li], wts["w_up"][li], wts["w_down"][li])

    hn = rmsnorm(h, prm["kv_norm_g"])
    nb = wts["w_k"].shape[1]
    n_hb = nb // HEAD
    k_new, k_bf = matmul(hn, wts["w_k"], out_dtypes=(F32, BF16))
    v_new, v_bf = matmul(hn, wts["w_v"], out_dtypes=(F32, BF16))
    logf = bias_log_sigmoid(matmul(hn, wts["w_f"], tk=4096), prm["b_f"])[:, :n_hb]
    if decode:
        k_cache, v_cache, logf_cache, table = past
        ck_past, total = paged_cumsum(logf_cache, table)
        k_cache = k_cache.reshape(-1, HEAD)
        v_cache = v_cache.reshape(-1, HEAD)
    else:
        npg = l // PAGE
        table = jnp.arange(b * npg, dtype=jnp.int32).reshape(b, npg)
        cum, _ = paged_cumsum(logf.reshape(b * npg, PAGE, n_hb), table)
        cum_t = cum.transpose(0, 2, 1).reshape(b, n_hb, 1, l)
        k_bf = k_bf.reshape(b, l, nb)
        v_bf = v_bf.reshape(b, l, nb)
    for j in range(prm["b_w_q"].shape[0]):
        li = n_a + j
        xn = rmsnorm(h, prm["norm_mix_g"][li])
        if decode:
            q = matmul(xn, wts["w_q"][j])
            hv = lambda t: t.reshape(b, n_hb, HEAD)
            o = fox_decode(hv(q), k_cache, v_cache, table, ck_past, total,
                           logf.reshape(b, 1, n_hb), hv(k_new), hv(v_new), n_hb).astype(BF16)
        else:
            q = matmul(xn, wts["w_q"][j], out_dtypes=(BF16,))
            o = fox_prompt(q.reshape(b, l, nb), k_bf, v_bf, cum, cum_t, n_hb)
        h = matmul(o.reshape(m, nb), wts["w_o"][j], res=h)
        h = _mlp(h, prm["norm_mlp_g"][li], wts["w_up"][li], wts["w_down"][li])
    y = rmsnorm(h, prm["final_norm_g"], out_dtype=F32)
    return (y.reshape(b, l, d), jnp.stack(new_delta), jnp.stack(new_conv),
            k_new.reshape(b, l, n_hb, HEAD), v_new.reshape(b, l, n_hb, HEAD),
            logf.reshape(b, l, n_hb))


def kernel(x_prompt, x_sample, cache_k, cache_v, cache_logf, state_delta, state_conv, page_table,
           norm_mix_g, norm_mlp_g, w_up, w_down, a_w_in, a_w_conv, a_log, a_dt_bias, a_o_norm_g,
           a_w_out, kv_norm_g, w_k, w_v, w_f, b_f, b_w_q, b_w_o, final_norm_g):
    conv_dim = a_w_conv.shape[-1]
    v_dim = a_w_out.shape[1]
    wts = _prep_weights(w_up, w_down, a_w_in, a_w_out, w_k, w_v, w_f, b_w_q, b_w_o, conv_dim, v_dim)
    prm = dict(norm_mix_g=norm_mix_g, norm_mlp_g=norm_mlp_g, a_w_conv=a_w_conv, a_log=a_log,
               a_dt_bias=a_dt_bias, a_o_norm_g=a_o_norm_g, kv_norm_g=kv_norm_g, b_f=b_f,
               b_w_q=b_w_q, final_norm_g=final_norm_g)
    n_a = a_log.shape[0]
    n_p = x_prompt.shape[0]
    n_h = a_log.shape[1]
    zero_conv = jnp.zeros((n_a, n_p, CONV_W - 1, conv_dim), state_conv.dtype)
    zero_delta = jnp.zeros((n_a, n_p, n_h, HEAD, HEAD), state_delta.dtype)
    y_p, delta_p, conv_p, k_p, v_p, logf_p = _trunk(x_prompt, zero_conv, zero_delta, None, wts, prm)
    y_s, delta_s, conv_s, k_s, v_s, logf_s = _trunk(
        x_sample, state_conv, state_delta, (cache_k, cache_v, cache_logf, page_table), wts, prm)
    return (y_p, y_s, delta_p, conv_p, k_p, v_p, logf_p, delta_s, conv_s, k_s, v_s, logf_s)
```

```python
import functools

import jax
import jax.numpy as jnp
from jax import lax
from jax.experimental import pallas as pl
from jax.experimental.pallas import tpu as pltpu

F32 = jnp.float32
BF16 = jnp.bfloat16

EPS = 1e-6
HEAD = 128
CHUNK = 64
CONV_W = 4
PAGE = 128
NEG = -1e30
V7X_VMEM_LIMIT = 56 * 1024 * 1024


def _cparams(sem):
    return pltpu.CompilerParams(dimension_semantics=sem, vmem_limit_bytes=V7X_VMEM_LIMIT)


def _rms_body(x_ref, g_ref, o_ref):
    x = x_ref[...]
    ms = jnp.mean(x * x, axis=-1, keepdims=True)
    o_ref[...] = (x * lax.rsqrt(ms + EPS) * g_ref[...]).astype(o_ref.dtype)


def rmsnorm(x, g, out_dtype=BF16):
    m, d = x.shape
    tm = min(m, 256)
    return pl.pallas_call(
        _rms_body,
        grid=(m // tm,),
        in_specs=[pl.BlockSpec((tm, d), lambda i: (i, 0)),
                  pl.BlockSpec((1, d), lambda i: (0, 0))],
        out_specs=pl.BlockSpec((tm, d), lambda i: (i, 0)),
        out_shape=jax.ShapeDtypeStruct((m, d), out_dtype),
        compiler_params=_cparams(("parallel",)),
        name="rmsnorm",
    )(x, g.reshape(1, d))


def _mm_body(*refs, nk, relu2, scale, has_res, n_out):
    a_ref, w_ref = refs[0], refs[1]
    pos = 2
    r_ref = None
    if has_res:
        r_ref = refs[pos]
        pos += 1
    o_refs = refs[pos:pos + n_out]
    scr = refs[pos + n_out:]

    def finish(acc):
        if relu2:
            acc = jnp.maximum(acc, 0.0)
            acc = acc * acc
        if scale is not None:
            acc = acc * scale
        if has_res:
            acc = r_ref[...] + acc
        for o_ref in o_refs:
            o_ref[...] = acc.astype(o_ref.dtype)

    part = jnp.dot(a_ref[...], w_ref[...], preferred_element_type=F32)
    if nk == 1:
        finish(part)
    else:
        acc_ref = scr[0]
        k = pl.program_id(2)

        @pl.when(k == 0)
        def _():
            acc_ref[...] = part

        @pl.when(jnp.logical_and(k > 0, k < nk - 1))
        def _():
            acc_ref[...] += part

        @pl.when(k == nk - 1)
        def _():
            finish(acc_ref[...] + part)


def matmul(a, w, out_dtypes=(F32,), res=None, relu2=False, scale=None, layer=None, n=None,
           tm=1024, tn=1024, tk=4096):
    m, kd = a.shape
    n = w.shape[-1] if n is None else n
    if kd > tk:
        tk //= 2
    tm, tn, tk = min(tm, m), min(tn, n), min(tk, kd)
    nk = kd // tk
    if layer is None:
        w_spec = pl.BlockSpec((tk, tn), lambda i, j, k: (k, j))
    else:
        w_spec = pl.BlockSpec((None, tk, tn), lambda i, j, k: (layer, k, j))
    in_specs = [pl.BlockSpec((tm, tk), lambda i, j, k: (i, k)), w_spec]
    args = [a, w]
    if res is not None:
        in_specs.append(pl.BlockSpec((tm, tn), lambda i, j, k: (i, j)))
        args.append(res)
    out = pl.pallas_call(
        functools.partial(_mm_body, nk=nk, relu2=relu2, scale=scale, has_res=res is not None,
                          n_out=len(out_dtypes)),
        grid=(m // tm, n // tn, nk),
        in_specs=in_specs,
        out_specs=[pl.BlockSpec((tm, tn), lambda i, j, k: (i, j)) for _ in out_dtypes],
        out_shape=[jax.ShapeDtypeStruct((m, n), dt) for dt in out_dtypes],
        scratch_shapes=[pltpu.VMEM((tm, tn), F32)] if nk > 1 else [],
        compiler_params=_cparams(("parallel", "parallel", "arbitrary")),
        name="matmul",
    )(*args)
    return out[0] if len(out_dtypes) == 1 else tuple(out)


def _softplus(x):
    return jnp.maximum(x, 0.0) + jnp.log1p(jnp.exp(-jnp.abs(x)))


def _gates_body(x_ref, alog_ref, dt_ref, o_ref, *, nh):
    x = x_ref[...]
    lane = lax.broadcasted_iota(jnp.int32, x.shape, 1)
    g = -jnp.exp(alog_ref[...]) * _softplus(x + dt_ref[...])
    beta = jax.nn.sigmoid(x)
    o_ref[...] = jnp.where(lane < nh, g, beta)


def gdn_gates(ab, a_log, dt_bias):
    m, w = ab.shape
    nh = a_log.shape[0]
    tm = min(m, 1024)
    pad = lambda v: jnp.pad(v.astype(F32), (0, w - nh)).reshape(1, w)
    return pl.pallas_call(
        functools.partial(_gates_body, nh=nh),
        grid=(m // tm,),
        in_specs=[pl.BlockSpec((tm, w), lambda i: (i, 0)),
                  pl.BlockSpec((1, w), lambda i: (0, 0)),
                  pl.BlockSpec((1, w), lambda i: (0, 0))],
        out_specs=pl.BlockSpec((tm, w), lambda i: (i, 0)),
        out_shape=jax.ShapeDtypeStruct((m, w), F32),
        compiler_params=_cparams(("parallel",)),
        name="gdn_gates",
    )(ab, pad(a_log), pad(dt_bias))


def _logsig_body(x_ref, b_ref, o_ref):
    y = x_ref[...] + b_ref[...]
    o_ref[...] = -_softplus(-y)


def bias_log_sigmoid(x, b):
    m, w = x.shape
    tm = min(m, 1024)
    return pl.pallas_call(
        _logsig_body,
        grid=(m // tm,),
        in_specs=[pl.BlockSpec((tm, w), lambda i: (i, 0)),
                  pl.BlockSpec((1, w), lambda i: (0, 0))],
        out_specs=pl.BlockSpec((tm, w), lambda i: (i, 0)),
        out_shape=jax.ShapeDtypeStruct((m, w), F32),
        compiler_params=_cparams(("parallel",)),
        name="bias_log_sigmoid",
    )(x, jnp.pad(b.astype(F32), (0, w - b.shape[0])).reshape(1, w))


def _l2_heads(y, o_ref, idx, qscale):
    for j in range(y.shape[-1] // HEAD):
        ys = y[..., j * HEAD:(j + 1) * HEAD]
        ss = jnp.sum(ys * ys, axis=-1, keepdims=True)
        o_ref[idx + (slice(j * HEAD, (j + 1) * HEAD),)] = ys * lax.rsqrt(ss + EPS) * qscale


def _conv_body(x_ref, buf_ref, w_ref, o_ref, ext_ref, *, tl, n_q_tiles):
    c = pl.program_id(1)
    t = pl.program_id(2)

    @pl.when(t == 0)
    def _():
        ext_ref[0:8, :] = buf_ref[0]

    u = x_ref[0]
    ext_ref[8:8 + tl, :] = u
    w = w_ref[...]
    acc = ext_ref[5:5 + tl, :] * w[0:1]
    acc = acc + ext_ref[6:6 + tl, :] * w[1:2]
    acc = acc + ext_ref[7:7 + tl, :] * w[2:3]
    acc = acc + u * w[3:4]
    y = acc * jax.nn.sigmoid(acc)

    @pl.when(c < 2 * n_q_tiles)
    def _():
        qscale = jnp.where(c < n_q_tiles, HEAD ** -0.5, 1.0).astype(F32)
        _l2_heads(y, o_ref, (0, slice(None)), qscale)

    @pl.when(c >= 2 * n_q_tiles)
    def _():
        o_ref[0] = y

    ext_ref[0:8, :] = ext_ref[tl:tl + 8, :]


def conv_prompt(proj, buf8, w_conv, qk_dim, tl=512, tc=512):
    b, l, _ = proj.shape
    cd = w_conv.shape[1]
    tl, tc = min(tl, l), min(tc, qk_dim)
    return pl.pallas_call(
        functools.partial(_conv_body, tl=tl, n_q_tiles=qk_dim // tc),
        grid=(b, cd // tc, l // tl),
        in_specs=[pl.BlockSpec((1, tl, tc), lambda i, c, t: (i, t, c)),
                  pl.BlockSpec((1, 8, tc), lambda i, c, t: (i, 0, c)),
                  pl.BlockSpec((CONV_W, tc), lambda i, c, t: (0, c))],
        out_specs=pl.BlockSpec((1, tl, tc), lambda i, c, t: (i, t, c)),
        out_shape=jax.ShapeDtypeStruct((b, l, cd), F32),
        scratch_shapes=[pltpu.VMEM((tl + 8, tc), F32)],
        compiler_params=_cparams(("parallel", "parallel", "arbitrary")),
        name="conv_prompt",
    )(proj, buf8, w_conv)


def _conv_dec_body(u_ref, buf_ref, w_ref, o_ref, *, n_q_tiles):
    c = pl.program_id(0)
    w = w_ref[...]
    acc = buf_ref[:, 0, :] * w[0:1]
    acc = acc + buf_ref[:, 1, :] * w[1:2]
    acc = acc + buf_ref[:, 2, :] * w[2:3]
    acc = acc + u_ref[...] * w[3:4]
    y = acc * jax.nn.sigmoid(acc)

    @pl.when(c < 2 * n_q_tiles)
    def _():
        qscale = jnp.where(c < n_q_tiles, HEAD ** -0.5, 1.0).astype(F32)
        _l2_heads(y, o_ref, (slice(None),), qscale)

    @pl.when(c >= 2 * n_q_tiles)
    def _():
        o_ref[...] = y


def conv_decode(proj, buf, w_conv, qk_dim, tc=512):
    b = proj.shape[0]
    cd = w_conv.shape[1]
    tc = min(tc, qk_dim)
    return pl.pallas_call(
        functools.partial(_conv_dec_body, n_q_tiles=qk_dim // tc),
        grid=(cd // tc,),
        in_specs=[pl.BlockSpec((b, tc), lambda c: (0, c)),
                  pl.BlockSpec((b, CONV_W - 1, tc), lambda c: (0, 0, c)),
                  pl.BlockSpec((CONV_W, tc), lambda c: (0, c))],
        out_specs=pl.BlockSpec((b, tc), lambda c: (0, c)),
        out_shape=jax.ShapeDtypeStruct((b, cd), F32),
        compiler_params=_cparams(("parallel",)),
        name="conv_decode",
    )(proj, buf, w_conv)


def _split(x):
    hi = x.astype(BF16)
    lo = (x - hi.astype(F32)).astype(BF16)
    return hi, lo


def _dot(a, b, dims=(((1,), (0,)), ((), ()))):
    return lax.dot_general(a, b, dims, preferred_element_type=F32)


_NT = (((1,), (1,)), ((), ()))
_TN = (((0,), (0,)), ((), ()))


def _dot1(a, b, dims=(((1,), (0,)), ((), ()))):
    return _dot(a.astype(BF16), b.astype(BF16), dims)


def _dot3(a, b, dims=(((1,), (0,)), ((), ()))):
    ah, al = _split(a)
    bh, bl = _split(b)
    return _dot(ah, bh, dims) + (_dot(ah, bl, dims) + _dot(al, bh, dims))


_BMM = (((2,), (1,)), ((0,), (0,)))
_BMM_NT = (((2,), (2,)), ((0,), (0,)))
_BMM_TN = (((1,), (1,)), ((0,), (0,)))
INV_PASSES = 1


def _bdot(a, b, dims=_BMM, passes=1):
    if passes == 1:
        return _dot(a.astype(BF16), b.astype(BF16), dims)
    ah, al = _split(a)
    bh, bl = _split(b)
    return _dot(ah, bh, dims) + (_dot(ah, bl, dims) + _dot(al, bh, dims))


def _split3(x):
    hi = x.astype(BF16)
    r1 = x - hi.astype(F32)
    mid = r1.astype(BF16)
    lo = (r1 - mid.astype(F32)).astype(BF16)
    return hi, mid, lo


def _unit_lower_inverse(a_strict, ii, jj):
    c = a_strict.shape[-1]
    eye = (ii == jj).astype(F32)
    m = None
    s = 1
    k = 0
    while s < c:
        off = jnp.logical_and((ii >> (k + 1)) == (jj >> (k + 1)),
                              jnp.logical_and(((ii >> k) & 1) == 1, ((jj >> k) & 1) == 0))
        a_off = jnp.where(off, a_strict, 0.0)
        if m is None:
            m = eye - a_off
        else:
            m = m - _bdot(m, _bdot(a_off, m, passes=INV_PASSES), passes=INV_PASSES)
        s *= 2
        k += 1
    return m


def _delta_body(q_ref, k_ref, v_ref, z_ref, g_ref, b_ref, s0_ref, gn_ref, o_ref, sf_ref, s_ref,
                *, hb, tl):
    t = pl.program_id(2)
    nt = pl.num_programs(2)
    c = CHUNK
    npc = tl // c

    @pl.when(t == 0)
    def _():
        s_ref[...] = s0_ref[0]

    ii = lax.broadcasted_iota(jnp.int32, (c, c), 0)
    jj = lax.broadcasted_iota(jnp.int32, (c, c), 1)
    tril = ii >= jj
    stril = ii > jj
    tril_b = tril.astype(BF16)
    triu_b = (ii <= jj).astype(BF16)
    eye_b = (ii == jj).astype(BF16)
    gn = gn_ref[...]

    row0 = pl.multiple_of(t * npc, npc)
    g_rows = jnp.concatenate([g_ref[0, hh, pl.ds(row0, npc), :] for hh in range(hb)], axis=0)
    b_rows = jnp.concatenate([b_ref[0, hh, pl.ds(row0, npc), :] for hh in range(hb)], axis=0)
    g3 = _split3(g_rows)
    b3 = _split3(b_rows)
    dcum_rows = _dot(g3[0], triu_b) + (_dot(g3[1], triu_b) + _dot(g3[2], triu_b))
    dcum_cols = _dot(tril_b, g3[0], _NT) + (_dot(tril_b, g3[1], _NT) + _dot(tril_b, g3[2], _NT))
    b_cols = _dot(eye_b, b3[0], _NT) + (_dot(eye_b, b3[1], _NT) + _dot(eye_b, b3[2], _NT))

    ks, qs, decs, rhss, dcs, bcs = [], [], [], [], [], []
    for j in range(npc):
        rs = slice(j * c, (j + 1) * c)
        for hh in range(hb):
            hs = slice(hh * HEAD, (hh + 1) * HEAD)
            gi = hh * npc + j
            k = k_ref[0, rs, hs]
            d_col = dcum_cols[:, gi:gi + 1]
            b_col = b_cols[:, gi:gi + 1]
            diff = d_col - dcum_rows[gi:gi + 1, :]
            decs.append(jnp.where(tril, jnp.exp(jnp.where(tril, diff, 0.0)), 0.0))
            rhss.append(jnp.concatenate([v_ref[0, rs, hs] * b_col,
                                         k * (b_col * jnp.exp(d_col))], axis=1))
            ks.append(k)
            qs.append(q_ref[0, rs, hs])
            dcs.append(d_col)
            bcs.append(b_col)
    kb = jnp.stack(ks)
    qb = jnp.stack(qs)
    dec = jnp.stack(decs)
    d_colb = jnp.stack(dcs)
    d_lastb = d_colb[:, c - 1:c, :]
    a_mat = jnp.stack(bcs) * _bdot(kb, kb, _BMM_NT, passes=INV_PASSES) * jnp.where(stril, dec, 0.0)
    sol = _bdot(_unit_lower_inverse(a_mat, ii, jj), jnp.stack(rhss), passes=INV_PASSES)
    attn = _bdot(qb, kb, _BMM_NT) * dec
    wq = jnp.concatenate([sol[:, :, HEAD:], qb * jnp.exp(d_colb)], axis=1)
    k_dec = kb * jnp.exp(d_lastb - d_colb)
    g_last = jnp.exp(d_lastb)

    s_mat = s_ref[...]
    for j in range(npc):
        ps = slice(j * hb, (j + 1) * hb)
        rs = slice(j * c, (j + 1) * c)
        wq_s = _bdot(wq[ps], s_mat)
        v_new = sol[ps, :, :HEAD] - wq_s[:, :c]
        o = wq_s[:, c:] + _bdot(attn[ps], v_new)
        s_mat = s_mat * g_last[ps] + _bdot(k_dec[ps], v_new, _BMM_TN)
        ms = jnp.mean(o * o, axis=-1, keepdims=True)
        on = o * lax.rsqrt(ms + EPS) * gn
        for hh in range(hb):
            hs = slice(hh * HEAD, (hh + 1) * HEAD)
            z = z_ref[0, rs, hs]
            o_ref[0, rs, hs] = (on[hh] * (z * jax.nn.sigmoid(z))).astype(o_ref.dtype)
    s_ref[...] = s_mat

    @pl.when(t == nt - 1)
    def _():
        sf_ref[0] = s_ref[...]


def delta_prompt(qkv, proj, g, beta, s0, gnorm, n_heads, hb=4, tl=512):
    b, l, cd = qkv.shape
    tl, hb = min(tl, l), min(hb, n_heads)
    nhg = n_heads // hb
    wb = hb * HEAD
    nc = l // CHUNK
    z_off = cd // wb
    return pl.pallas_call(
        functools.partial(_delta_body, hb=hb, tl=tl),
        grid=(b, nhg, l // tl),
        in_specs=[pl.BlockSpec((1, tl, wb), lambda i, h, t: (i, t, h)),
                  pl.BlockSpec((1, tl, wb), lambda i, h, t: (i, t, nhg + h)),
                  pl.BlockSpec((1, tl, wb), lambda i, h, t: (i, t, 2 * nhg + h)),
                  pl.BlockSpec((1, tl, wb), lambda i, h, t: (i, t, z_off + h)),
                  pl.BlockSpec((1, hb, nc, CHUNK), lambda i, h, t: (i, h, 0, 0)),
                  pl.BlockSpec((1, hb, nc, CHUNK), lambda i, h, t: (i, h, 0, 0)),
                  pl.BlockSpec((1, hb, HEAD, HEAD), lambda i, h, t: (i, h, 0, 0)),
                  pl.BlockSpec((1, HEAD), lambda i, h, t: (0, 0))],
        out_specs=[pl.BlockSpec((1, tl, wb), lambda i, h, t: (i, t, h)),
                   pl.BlockSpec((1, hb, HEAD, HEAD), lambda i, h, t: (i, h, 0, 0))],
        out_shape=[jax.ShapeDtypeStruct((b, l, n_heads * HEAD), BF16),
                   jax.ShapeDtypeStruct((b, n_heads, HEAD, HEAD), F32)],
        scratch_shapes=[pltpu.VMEM((hb, HEAD, HEAD), F32)],
        compiler_params=_cparams(("parallel", "parallel", "arbitrary")),
        name="delta_prompt",
    )(qkv, qkv, qkv, proj, g, beta, s0, gnorm.reshape(1, HEAD))


def _delta_dec_body(q_ref, k_ref, v_ref, z_ref, g_ref, b_ref, s0_ref, gn_ref, o_ref, sf_ref, *, nh):
    ii = lax.broadcasted_iota(jnp.int32, (HEAD, HEAD), 0)
    jj = lax.broadcasted_iota(jnp.int32, (HEAD, HEAD), 1)
    eye = ii == jj
    gn = gn_ref[...]

    def col_of(row):
        return jnp.sum(jnp.where(eye, jnp.broadcast_to(row, (HEAD, HEAD)), 0.0), axis=1, keepdims=True)

    def head(h, carry):
        q = q_ref[0, pl.ds(h, 1), :]
        k = k_ref[0, pl.ds(h, 1), :]
        v = v_ref[0, pl.ds(h, 1), :]
        z = z_ref[0, pl.ds(h, 1), :]
        eg = jnp.exp(g_ref[0, pl.ds(h, 1), :])
        beta = b_ref[0, pl.ds(h, 1), :]
        s_mat = s0_ref[0, h]
        k_col = col_of(k)
        ks = jnp.sum(k_col * s_mat, axis=0, keepdims=True)
        qs = jnp.sum(col_of(q) * s_mat, axis=0, keepdims=True)
        v_new = v * beta - (beta * eg) * ks
        qk = jnp.sum(q * k, axis=-1, keepdims=True)
        o = eg * qs + qk * v_new
        sf_ref[0, h] = s_mat * eg + k_col * v_new
        ms = jnp.mean(o * o, axis=-1, keepdims=True)
        on = o * lax.rsqrt(ms + EPS) * gn
        o_ref[0, pl.ds(h, 1), :] = (on * (z * jax.nn.sigmoid(z))).astype(o_ref.dtype)
        return carry

    lax.fori_loop(0, nh, head, 0)


def delta_decode(q, k, v, z, g_rep, b_rep, s0, gnorm):
    b, nh, _ = q.shape
    vec = pl.BlockSpec((1, nh, HEAD), lambda i: (i, 0, 0))
    st = pl.BlockSpec((1, nh, HEAD, HEAD), lambda i: (i, 0, 0, 0))
    return pl.pallas_call(
        functools.partial(_delta_dec_body, nh=nh),
        grid=(b,),
        in_specs=[vec, vec, vec, vec, vec, vec, st, pl.BlockSpec((1, HEAD), lambda i: (0, 0))],
        out_specs=[vec, st],
        out_shape=[jax.ShapeDtypeStruct((b, nh, HEAD), F32),
                   jax.ShapeDtypeStruct((b, nh, HEAD, HEAD), F32)],
        compiler_params=_cparams(("parallel",)),
        name="delta_decode",
    )(q, k, v, z, g_rep, b_rep, s0, gnorm.reshape(1, HEAD))


def _cumsum_body(*refs, pps):
    x_refs = refs[1:1 + pps]
    o_ref, tot_ref, carry_ref = refs[1 + pps:]
    p = pl.program_id(1)
    npg = pl.num_programs(1)

    @pl.when(p == 0)
    def _():
        carry_ref[...] = jnp.zeros_like(carry_ref)

    n = x_refs[0].shape[1]
    tri = (lax.broadcasted_iota(jnp.int32, (n, n), 0)
           >= lax.broadcasted_iota(jnp.int32, (n, n), 1)).astype(BF16)
    carry = carry_ref[...]
    for r, x_ref in enumerate(x_refs):
        hi, mid, lo = _split3(x_ref[0])
        out = (_dot(tri, hi) + (_dot(tri, mid) + _dot(tri, lo))) + carry
        o_ref[0, r * n:(r + 1) * n, :] = out
        carry = out[n - 1:n, :]
    carry_ref[...] = carry

    @pl.when(p == npg - 1)
    def _():
        tot_ref[0] = carry


def paged_cumsum(pages, table):
    b, npg = table.shape
    _, pg, h = pages.shape
    pps = next(c for c in (8, 4, 2, 1) if npg % c == 0)
    page_spec = lambda r: pl.BlockSpec((1, pg, h), lambda i, p, pt: (pt[i, p * pps + r], 0, 0))
    return pl.pallas_call(
        functools.partial(_cumsum_body, pps=pps),
        grid_spec=pltpu.PrefetchScalarGridSpec(
            num_scalar_prefetch=1,
            grid=(b, npg // pps),
            in_specs=[page_spec(r) for r in range(pps)],
            out_specs=[pl.BlockSpec((1, pps * pg, h), lambda i, p, pt: (i, p, 0)),
                       pl.BlockSpec((1, 1, h), lambda i, p, pt: (i, 0, 0))],
            scratch_shapes=[pltpu.VMEM((1, h), F32)]),
        out_shape=[jax.ShapeDtypeStruct((b, npg * pg, h), F32),
                   jax.ShapeDtypeStruct((b, 1, h), F32)],
        compiler_params=_cparams(("parallel", "arbitrary")),
        name="paged_cumsum",
    )(table, *([pages] * pps))


def _lane_pack3(x, nh):
    pieces = _split3(x)
    row = lax.broadcasted_iota(jnp.int32, (nh, HEAD), 0)
    lane = lax.broadcasted_iota(jnp.int32, (nh, HEAD), 1)
    out = None
    for gi, pc in enumerate(pieces):
        part = _dot(pc, (lane == gi * nh + row).astype(BF16))
        out = part if out is None else out + part
    return out


def _bias_cols_body(cum_ref, qx_ref, kx_ref, *, nh):
    h = pl.program_id(1)
    packed = _lane_pack3(cum_ref[0], nh).astype(BF16)
    r = lax.broadcasted_iota(jnp.int32, (HEAD, HEAD), 0)
    j = lax.broadcasted_iota(jnp.int32, (HEAD, HEAD), 1)
    piece = lax.div(r, nh)
    mine = jnp.logical_and(lax.rem(r, nh) == h, piece < 3)
    to_k = jnp.logical_and(mine, j == piece).astype(BF16)
    to_q = jnp.logical_and(mine, j == piece + 3).astype(BF16)
    lane = lax.broadcasted_iota(jnp.int32, (1, HEAD), 1)
    kx_ref[0, 0] = (jnp.logical_and(lane >= 3, lane < 6).astype(F32) - _dot(packed, to_k)).astype(BF16)
    qx_ref[0, 0] = ((lane < 3).astype(F32) + _dot(packed, to_q)).astype(BF16)


def fox_bias_cols(cum, n_heads):
    b, l, _ = cum.shape
    out = pl.BlockSpec((1, 1, l, HEAD), lambda i, h: (i, h, 0, 0))
    return pl.pallas_call(
        functools.partial(_bias_cols_body, nh=n_heads),
        grid=(b, n_heads),
        in_specs=[pl.BlockSpec((1, l, n_heads), lambda i, h: (i, 0, 0))],
        out_specs=[out, out],
        out_shape=[jax.ShapeDtypeStruct((b, n_heads, l, HEAD), BF16)] * 2,
        compiler_params=_cparams(("parallel", "parallel")),
        name="fox_bias_cols",
    )(cum)


def _flash_body(q_ref, k_ref, v_ref, qx_ref, kx_ref, o_ref, *, tq):
    nq = q_ref.shape[1] // tq
    causal = (lax.broadcasted_iota(jnp.int32, (tq, tq), 0)
              >= lax.broadcasted_iota(jnp.int32, (tq, tq), 1))
    for qi in range(nq):
        rq = slice(qi * tq, (qi + 1) * tq)
        qa = jnp.concatenate([q_ref[0, rq, :], qx_ref[0, 0, rq, :]], axis=1)
        m = jnp.full((tq, 1), NEG, F32)
        lsum = jnp.zeros((tq, 1), F32)
        acc = jnp.zeros((tq, HEAD), F32)
        for ki in range(qi + 1):
            rk = slice(ki * tq, (ki + 1) * tq)
            ka = jnp.concatenate([k_ref[0, rk, :], kx_ref[0, 0, rk, :]], axis=1)
            s = _dot(qa, ka, _NT)
            if ki == qi:
                s = jnp.where(causal, s, NEG)
            m_new = jnp.maximum(m, jnp.max(s, axis=1, keepdims=True))
            alpha = jnp.exp(m - m_new)
            p = jnp.exp(s - m_new)
            lsum = alpha * lsum + jnp.sum(p, axis=1, keepdims=True)
            acc = alpha * acc + _dot(p.astype(BF16), v_ref[0, rk, :])
            m = m_new
        o_ref[0, rq, :] = (acc / lsum).astype(o_ref.dtype)


def fox_prompt(q, k, v, qx, kx, n_heads, tq=512):
    b, l, _ = q.shape
    tq = min(tq, l)
    tok = pl.BlockSpec((1, l, HEAD), lambda i, h: (i, 0, h))
    col = pl.BlockSpec((1, 1, l, HEAD), lambda i, h: (i, h, 0, 0))
    return pl.pallas_call(
        functools.partial(_flash_body, tq=tq),
        grid=(b, n_heads),
        in_specs=[tok, tok, tok, col, col],
        out_specs=tok,
        out_shape=jax.ShapeDtypeStruct(q.shape, BF16),
        compiler_params=_cparams(("parallel", "parallel")),
        name="fox_prompt",
    )(q, k, v, qx, kx)


def _fox_dec_body(pt_ref, q_ref, kc_ref, vc_ref, ck_ref, tot_ref, lf_ref, kn_ref, vn_ref,
                  o_ref, m_ref, l_ref, acc_ref, *, scale, nh):
    del pt_ref
    p = pl.program_id(1)
    npg = pl.num_programs(1)

    @pl.when(p == 0)
    def _():
        m_ref[...] = jnp.full_like(m_ref, NEG)
        l_ref[...] = jnp.zeros_like(l_ref)
        acc_ref[...] = jnp.zeros_like(acc_ref)

    q = q_ref[0]
    ones = jnp.ones((HEAD, HEAD), BF16)
    sub = lax.broadcasted_iota(jnp.int32, (nh, HEAD), 0)
    lane = lax.broadcasted_iota(jnp.int32, (nh, HEAD), 1)
    own = jnp.logical_and(lax.rem(lane, nh) == sub, lane < 3 * nh)

    def head_rows(packed):
        r = packed.shape[0]
        z = jnp.where(own[None], jnp.broadcast_to(packed[:, None, :], (r, nh, HEAD)), 0.0)
        return _dot(z.reshape(r * nh, HEAD).astype(BF16), ones).reshape(r, nh, HEAD)

    cq = head_rows(_lane_pack3(jnp.broadcast_to(tot_ref[0] + lf_ref[0], (8, nh)), nh))[0]
    ck = head_rows(_lane_pack3(ck_ref[0], nh))
    kc = kc_ref[...].reshape(PAGE, nh, HEAD)
    vc = vc_ref[...].reshape(PAGE, nh, HEAD)
    qk = _dot((kc * q[None]).reshape(PAGE * nh, HEAD).astype(BF16), ones).reshape(PAGE, nh, HEAD)
    s = qk * scale + (cq[None] - ck)
    m_prev = m_ref[...]
    m_new = jnp.maximum(m_prev, jnp.max(s, axis=0))
    alpha = jnp.exp(m_prev - m_new)
    pe = jnp.exp(s - m_new[None])
    l_new = alpha * l_ref[...] + jnp.sum(pe, axis=0)
    acc_new = alpha * acc_ref[...] + jnp.sum(pe * vc, axis=0)
    m_ref[...] = m_new
    l_ref[...] = l_new
    acc_ref[...] = acc_new

    @pl.when(p == npg - 1)
    def _():
        s_n = jnp.sum(kn_ref[0] * q, axis=-1, keepdims=True) * scale + (cq - cq)
        m_fin = jnp.maximum(m_new, s_n)
        a_fin = jnp.exp(m_new - m_fin)
        p_n = jnp.exp(s_n - m_fin)
        l_fin = a_fin * l_new + p_n
        o_ref[0] = (a_fin * acc_new + p_n * vn_ref[0]) / l_fin


def fox_decode(q, k_cache, v_cache, table, ck_past, total, logf_new, k_new, v_new, n_heads):
    b, npg = table.shape
    rows = PAGE * n_heads
    vec = pl.BlockSpec((1, n_heads, HEAD), lambda i, p, pt: (i, 0, 0))
    row = pl.BlockSpec((1, 1, n_heads), lambda i, p, pt: (i, 0, 0))
    page = pl.BlockSpec((rows, HEAD), lambda i, p, pt: (pt[i, p], 0))
    return pl.pallas_call(
        functools.partial(_fox_dec_body, scale=HEAD ** -0.5, nh=n_heads),
        grid_spec=pltpu.PrefetchScalarGridSpec(
            num_scalar_prefetch=1,
            grid=(b, npg),
            in_specs=[vec, page, page,
                      pl.BlockSpec((1, PAGE, n_heads), lambda i, p, pt: (i, p, 0)),
                      row, row, vec, vec],
            out_specs=vec,
            scratch_shapes=[pltpu.VMEM((n_heads, HEAD), F32), pltpu.VMEM((n_heads, HEAD), F32),
                            pltpu.VMEM((n_heads, HEAD), F32)]),
        out_shape=jax.ShapeDtypeStruct((b, n_heads, HEAD), F32),
        compiler_params=_cparams(("parallel", "arbitrary")),
        name="fox_decode",
    )(table, q, k_cache, v_cache, ck_past, total, logf_new, k_new, v_new)


def _prep_weights(w_up, w_down, a_w_in, a_w_out, w_k, w_v, w_f, b_w_q, b_w_o, conv_dim, v_dim):
    n_main = conv_dim + v_dim
    n_ab = a_w_in.shape[-1] - n_main
    return dict(
        w_up=w_up.astype(BF16), w_down=w_down.astype(BF16),
        w_in=a_w_in.astype(BF16), n_main=n_main,
        w_ab=jnp.pad(a_w_in[:, :, n_main:], ((0, 0), (0, 0), (0, HEAD - n_ab))).astype(BF16),
        w_out=a_w_out.astype(BF16),
        w_k=w_k.astype(BF16), w_v=w_v.astype(BF16),
        w_f=jnp.pad(w_f, ((0, 0), (0, HEAD - w_f.shape[1]))).astype(BF16),
        w_q=b_w_q.astype(BF16), w_o=b_w_o.astype(BF16))


def _mlp(h, g, wts, li):
    hid = matmul(rmsnorm(h, g), wts["w_up"], layer=li, out_dtypes=(BF16,), relu2=True)
    return matmul(hid, wts["w_down"], layer=li, res=h)


def _trunk(x, conv_state, delta_state, past, wts, prm):
    b, l, d = x.shape
    m = b * l
    n_a = prm["a_log"].shape[0]
    n_h = prm["a_log"].shape[1]
    qk_dim = n_h * HEAD
    conv_dim = prm["a_w_conv"].shape[-1]
    decode = past is not None
    h = x.reshape(m, d)
    new_conv, new_delta = [], []
    for li in range(n_a):
        xn = rmsnorm(h, prm["norm_mix_g"][li])
        proj = matmul(xn, wts["w_in"], layer=li, n=wts["n_main"])
        ab = matmul(xn, wts["w_ab"], layer=li)
        gb = gdn_gates(ab, prm["a_log"][li], prm["a_dt_bias"][li])
        g, beta = gb[:, :n_h], gb[:, n_h:2 * n_h]
        if decode:
            buf = conv_state[li]
            qkv = conv_decode(proj, buf, prm["a_w_conv"][li], qk_dim)
            new_conv.append(jnp.concatenate([buf[:, 1:], proj[:, None, :conv_dim]], axis=1))
            hv = lambda t: t.reshape(b, n_h, HEAD)
            rep = lambda t: jnp.broadcast_to(t[:, :, None], (b, n_h, HEAD))
            o, s_new = delta_decode(hv(qkv[:, :qk_dim]), hv(qkv[:, qk_dim:2 * qk_dim]),
                                    hv(qkv[:, 2 * qk_dim:]), hv(proj[:, conv_dim:]),
                                    rep(g), rep(beta), delta_state[li], prm["a_o_norm_g"][li])
            o = o.reshape(m, n_h * HEAD).astype(BF16)
        else:
            proj3 = proj.reshape(b, l, -1)
            buf8 = jnp.pad(conv_state[li], ((0, 0), (8 - (CONV_W - 1), 0), (0, 0)))
            qkv = conv_prompt(proj3, buf8, prm["a_w_conv"][li], qk_dim)
            new_conv.append(proj3[:, l - (CONV_W - 1):, :conv_dim])
            chunks = lambda t: t.reshape(b, l // CHUNK, CHUNK, n_h).transpose(0, 3, 1, 2)
            o, s_new = delta_prompt(qkv, proj3, chunks(g), chunks(beta), delta_state[li],
                                    prm["a_o_norm_g"][li], n_h)
            o = o.reshape(m, n_h * HEAD)
        new_delta.append(s_new)
        h = matmul(o, wts["w_out"], layer=li, res=h)
        h = _mlp(h, prm["norm_mlp_g"][li], wts, li)

    hn = rmsnorm(h, prm["kv_norm_g"])
    nb = wts["w_k"].shape[1]
    n_hb = nb // HEAD
    k_new, k_bf = matmul(hn, wts["w_k"], out_dtypes=(F32, BF16))
    v_new, v_bf = matmul(hn, wts["w_v"], out_dtypes=(F32, BF16))
    logf = bias_log_sigmoid(matmul(hn, wts["w_f"]), prm["b_f"])[:, :n_hb]
    if decode:
        k_cache, v_cache, logf_cache, table = past
        ck_past, total = paged_cumsum(logf_cache, table)
        k_cache = k_cache.reshape(-1, HEAD)
        v_cache = v_cache.reshape(-1, HEAD)
    else:
        npg = l // PAGE
        table = jnp.arange(b * npg, dtype=jnp.int32).reshape(b, npg)
        cum, _ = paged_cumsum(logf.reshape(b * npg, PAGE, n_hb), table)
        qx, kx = fox_bias_cols(cum, n_hb)
        k_bf = k_bf.reshape(b, l, nb)
        v_bf = v_bf.reshape(b, l, nb)
    for j in range(prm["b_w_q"].shape[0]):
        li = n_a + j
        xn = rmsnorm(h, prm["norm_mix_g"][li])
        if decode:
            q = matmul(xn, wts["w_q"], layer=j)
            hv = lambda t: t.reshape(b, n_hb, HEAD)
            o = fox_decode(hv(q), k_cache, v_cache, table, ck_past, total,
                           logf.reshape(b, 1, n_hb), hv(k_new), hv(v_new), n_hb).astype(BF16)
        else:
            q = matmul(xn, wts["w_q"], layer=j, out_dtypes=(BF16,), scale=HEAD ** -0.5)
            o = fox_prompt(q.reshape(b, l, nb), k_bf, v_bf, qx, kx, n_hb)
        h = matmul(o.reshape(m, nb), wts["w_o"], layer=j, res=h)
        h = _mlp(h, prm["norm_mlp_g"][li], wts, li)
    y = rmsnorm(h, prm["final_norm_g"], out_dtype=F32)
    return (y.reshape(b, l, d), jnp.stack(new_delta), jnp.stack(new_conv),
            k_new.reshape(b, l, n_hb, HEAD), v_new.reshape(b, l, n_hb, HEAD),
            logf.reshape(b, l, n_hb))


def kernel(x_prompt, x_sample, cache_k, cache_v, cache_logf, state_delta, state_conv, page_table,
           norm_mix_g, norm_mlp_g, w_up, w_down, a_w_in, a_w_conv, a_log, a_dt_bias, a_o_norm_g,
           a_w_out, kv_norm_g, w_k, w_v, w_f, b_f, b_w_q, b_w_o, final_norm_g):
    conv_dim = a_w_conv.shape[-1]
    v_dim = a_w_out.shape[1]
    wts = _prep_weights(w_up, w_down, a_w_in, a_w_out, w_k, w_v, w_f, b_w_q, b_w_o, conv_dim, v_dim)
    prm = dict(norm_mix_g=norm_mix_g, norm_mlp_g=norm_mlp_g, a_w_conv=a_w_conv, a_log=a_log,
               a_dt_bias=a_dt_bias, a_o_norm_g=a_o_norm_g, kv_norm_g=kv_norm_g, b_f=b_f,
               b_w_q=b_w_q, final_norm_g=final_norm_g)
    n_a = a_log.shape[0]
    n_p = x_prompt.shape[0]
    n_h = a_log.shape[1]
    zero_conv = jnp.zeros((n_a, n_p, CONV_W - 1, conv_dim), state_conv.dtype)
    zero_delta = jnp.zeros((n_a, n_p, n_h, HEAD, HEAD), state_delta.dtype)
    y_p, delta_p, conv_p, k_p, v_p, logf_p = _trunk(x_prompt, zero_conv, zero_delta, None, wts, prm)
    y_s, delta_s, conv_s, k_s, v_s, logf_s = _trunk(
        x_sample, state_conv, state_delta, (cache_k, cache_v, cache_logf, page_table), wts, prm)
    return (y_p, y_s, delta_p, conv_p, k_p, v_p, logf_p, delta_s, conv_s, k_s, v_s, logf_s)
```

```python
import functools

import jax
import jax.numpy as jnp
from jax import lax
from jax.experimental import pallas as pl
from jax.experimental.pallas import tpu as pltpu

F32 = jnp.float32
BF16 = jnp.bfloat16

EPS = 1e-6
HEAD = 128
CHUNK = 64
CONV_W = 4
PAGE = 128
NEG = -1e30
V7X_VMEM_LIMIT = 56 * 1024 * 1024


def _cparams(sem):
    return pltpu.CompilerParams(dimension_semantics=sem, vmem_limit_bytes=V7X_VMEM_LIMIT)


def _rms_body(*refs, has_add, emit_sum):
    x = refs[0][...]
    if has_add:
        x = x + refs[1][...]
    g_ref = refs[1 + has_add]
    outs = refs[2 + has_add:]
    ms = jnp.mean(x * x, axis=-1, keepdims=True)
    outs[0][...] = (x * lax.rsqrt(ms + EPS) * g_ref[...]).astype(outs[0].dtype)
    if emit_sum:
        outs[1][...] = x


def rmsnorm(x, g, out_dtype=BF16, add=None, emit_sum=False):
    m, d = x.shape
    tm = min(m, 256)
    row = pl.BlockSpec((tm, d), lambda i: (i, 0))
    ins = [x] + ([add] if add is not None else [])
    out = pl.pallas_call(
        functools.partial(_rms_body, has_add=add is not None, emit_sum=emit_sum),
        grid=(m // tm,),
        in_specs=[row] * len(ins) + [pl.BlockSpec((1, d), lambda i: (0, 0))],
        out_specs=[row] * (1 + emit_sum),
        out_shape=[jax.ShapeDtypeStruct((m, d), out_dtype)]
        + ([jax.ShapeDtypeStruct((m, d), F32)] if emit_sum else []),
        compiler_params=_cparams(("parallel",)),
        name="rmsnorm",
    )(*ins, g.reshape(1, d))
    return tuple(out) if emit_sum else out[0]


def _mm_body(*refs, nk, relu2, scale, has_res, n_out, emit_w):
    a_ref, w_ref = refs[0], refs[1]
    pos = 2
    r_ref = None
    if has_res:
        r_ref = refs[pos]
        pos += 1
    o_refs = refs[pos:pos + n_out]
    pos += n_out
    w = w_ref[...]
    if emit_w:
        w = w.astype(BF16)
        refs[pos][...] = w
        pos += 1
    scr = refs[pos:]

    def finish(acc):
        if relu2:
            acc = jnp.maximum(acc, 0.0)
            acc = acc * acc
        if scale is not None:
            acc = acc * scale
        if has_res:
            acc = r_ref[...] + acc
        for o_ref in o_refs:
            o_ref[...] = acc.astype(o_ref.dtype)

    part = jnp.dot(a_ref[...], w, preferred_element_type=F32)
    if nk == 1:
        finish(part)
    else:
        acc_ref = scr[0]
        k = pl.program_id(2)

        @pl.when(k == 0)
        def _():
            acc_ref[...] = part

        @pl.when(jnp.logical_and(k > 0, k < nk - 1))
        def _():
            acc_ref[...] += part

        @pl.when(k == nk - 1)
        def _():
            finish(acc_ref[...] + part)


def matmul(a, w, out_dtypes=(F32,), res=None, relu2=False, scale=None, layer=None, n=None,
           emit_w=False, tm=1024, tn=1024, tk=4096):
    m, kd = a.shape
    n = w.shape[-1] if n is None else n
    if kd > tk and (res is not None or emit_w):
        tk //= 2
    tm, tn, tk = min(tm, m), min(tn, n), min(tk, kd)
    nk = kd // tk
    assert not emit_w or m == tm
    if layer is None:
        w_spec = pl.BlockSpec((tk, tn), lambda i, j, k: (k, j))
    else:
        w_spec = pl.BlockSpec((None, tk, tn), lambda i, j, k: (layer, k, j))
    in_specs = [pl.BlockSpec((tm, tk), lambda i, j, k: (i, k)), w_spec]
    args = [a, w]
    if res is not None:
        in_specs.append(pl.BlockSpec((tm, tn), lambda i, j, k: (i, j)))
        args.append(res)
    out_specs = [pl.BlockSpec((tm, tn), lambda i, j, k: (i, j)) for _ in out_dtypes]
    out_shape = [jax.ShapeDtypeStruct((m, n), dt) for dt in out_dtypes]
    if emit_w:
        out_specs.append(pl.BlockSpec((tk, tn), lambda i, j, k: (k, j)))
        out_shape.append(jax.ShapeDtypeStruct((kd, n), BF16))
    out = pl.pallas_call(
        functools.partial(_mm_body, nk=nk, relu2=relu2, scale=scale, has_res=res is not None,
                          n_out=len(out_dtypes), emit_w=emit_w),
        grid=(m // tm, n // tn, nk),
        in_specs=in_specs,
        out_specs=out_specs,
        out_shape=out_shape,
        scratch_shapes=[pltpu.VMEM((tm, tn), F32)] if nk > 1 else [],
        compiler_params=_cparams(("parallel", "parallel", "arbitrary")),
        name="matmul",
    )(*args)
    return out[0] if len(out) == 1 else tuple(out)


def _softplus(x):
    return jnp.maximum(x, 0.0) + jnp.log1p(jnp.exp(-jnp.abs(x)))


def _gates_body(x_ref, alog_ref, dt_ref, o_ref, *, nh):
    x = x_ref[...]
    lane = lax.broadcasted_iota(jnp.int32, x.shape, 1)
    g = -jnp.exp(alog_ref[...]) * _softplus(x + dt_ref[...])
    beta = jax.nn.sigmoid(x)
    o_ref[...] = jnp.where(lane < nh, g, beta)


def gdn_gates(ab, a_log, dt_bias):
    m, w = ab.shape
    nh = a_log.shape[0]
    tm = min(m, 1024)
    pad = lambda v: jnp.pad(v.astype(F32), (0, w - nh)).reshape(1, w)
    return pl.pallas_call(
        functools.partial(_gates_body, nh=nh),
        grid=(m // tm,),
        in_specs=[pl.BlockSpec((tm, w), lambda i: (i, 0)),
                  pl.BlockSpec((1, w), lambda i: (0, 0)),
                  pl.BlockSpec((1, w), lambda i: (0, 0))],
        out_specs=pl.BlockSpec((tm, w), lambda i: (i, 0)),
        out_shape=jax.ShapeDtypeStruct((m, w), F32),
        compiler_params=_cparams(("parallel",)),
        name="gdn_gates",
    )(ab, pad(a_log), pad(dt_bias))


def _logsig_body(x_ref, b_ref, o_ref):
    y = x_ref[...] + b_ref[...]
    o_ref[...] = -_softplus(-y)


def bias_log_sigmoid(x, b):
    m, w = x.shape
    tm = min(m, 1024)
    return pl.pallas_call(
        _logsig_body,
        grid=(m // tm,),
        in_specs=[pl.BlockSpec((tm, w), lambda i: (i, 0)),
                  pl.BlockSpec((1, w), lambda i: (0, 0))],
        out_specs=pl.BlockSpec((tm, w), lambda i: (i, 0)),
        out_shape=jax.ShapeDtypeStruct((m, w), F32),
        compiler_params=_cparams(("parallel",)),
        name="bias_log_sigmoid",
    )(x, jnp.pad(b.astype(F32), (0, w - b.shape[0])).reshape(1, w))


def _l2_heads(y, o_ref, idx, qscale):
    for j in range(y.shape[-1] // HEAD):
        ys = y[..., j * HEAD:(j + 1) * HEAD]
        ss = jnp.sum(ys * ys, axis=-1, keepdims=True)
        o_ref[idx + (slice(j * HEAD, (j + 1) * HEAD),)] = ys * lax.rsqrt(ss + EPS) * qscale


def _conv_body(x_ref, buf_ref, w_ref, o_ref, ext_ref, *, tl, n_q_tiles):
    c = pl.program_id(1)
    t = pl.program_id(2)

    @pl.when(t == 0)
    def _():
        ext_ref[0:8, :] = buf_ref[0]

    u = x_ref[0]
    ext_ref[8:8 + tl, :] = u
    w = w_ref[...]
    acc = ext_ref[5:5 + tl, :] * w[0:1]
    acc = acc + ext_ref[6:6 + tl, :] * w[1:2]
    acc = acc + ext_ref[7:7 + tl, :] * w[2:3]
    acc = acc + u * w[3:4]
    y = acc * jax.nn.sigmoid(acc)

    @pl.when(c < 2 * n_q_tiles)
    def _():
        qscale = jnp.where(c < n_q_tiles, HEAD ** -0.5, 1.0).astype(F32)
        _l2_heads(y, o_ref, (0, slice(None)), qscale)

    @pl.when(c >= 2 * n_q_tiles)
    def _():
        o_ref[0] = y

    ext_ref[0:8, :] = ext_ref[tl:tl + 8, :]


def conv_prompt(proj, buf8, w_conv, qk_dim, tl=512, tc=512):
    b, l, _ = proj.shape
    cd = w_conv.shape[1]
    tl, tc = min(tl, l), min(tc, qk_dim)
    return pl.pallas_call(
        functools.partial(_conv_body, tl=tl, n_q_tiles=qk_dim // tc),
        grid=(b, cd // tc, l // tl),
        in_specs=[pl.BlockSpec((1, tl, tc), lambda i, c, t: (i, t, c)),
                  pl.BlockSpec((1, 8, tc), lambda i, c, t: (i, 0, c)),
                  pl.BlockSpec((CONV_W, tc), lambda i, c, t: (0, c))],
        out_specs=pl.BlockSpec((1, tl, tc), lambda i, c, t: (i, t, c)),
        out_shape=jax.ShapeDtypeStruct((b, l, cd), F32),
        scratch_shapes=[pltpu.VMEM((tl + 8, tc), F32)],
        compiler_params=_cparams(("parallel", "parallel", "arbitrary")),
        name="conv_prompt",
    )(proj, buf8, w_conv)


def _conv_dec_body(u_ref, buf_ref, w_ref, o_ref, *, n_q_tiles):
    c = pl.program_id(0)
    w = w_ref[...]
    acc = buf_ref[:, 0, :] * w[0:1]
    acc = acc + buf_ref[:, 1, :] * w[1:2]
    acc = acc + buf_ref[:, 2, :] * w[2:3]
    acc = acc + u_ref[...] * w[3:4]
    y = acc * jax.nn.sigmoid(acc)

    @pl.when(c < 2 * n_q_tiles)
    def _():
        qscale = jnp.where(c < n_q_tiles, HEAD ** -0.5, 1.0).astype(F32)
        _l2_heads(y, o_ref, (slice(None),), qscale)

    @pl.when(c >= 2 * n_q_tiles)
    def _():
        o_ref[...] = y


def conv_decode(proj, buf, w_conv, qk_dim, tc=512):
    b = proj.shape[0]
    cd = w_conv.shape[1]
    tc = min(tc, qk_dim)
    return pl.pallas_call(
        functools.partial(_conv_dec_body, n_q_tiles=qk_dim // tc),
        grid=(cd // tc,),
        in_specs=[pl.BlockSpec((b, tc), lambda c: (0, c)),
                  pl.BlockSpec((b, CONV_W - 1, tc), lambda c: (0, 0, c)),
                  pl.BlockSpec((CONV_W, tc), lambda c: (0, c))],
        out_specs=pl.BlockSpec((b, tc), lambda c: (0, c)),
        out_shape=jax.ShapeDtypeStruct((b, cd), F32),
        compiler_params=_cparams(("parallel",)),
        name="conv_decode",
    )(proj, buf, w_conv)


def _split(x):
    hi = x.astype(BF16)
    lo = (x - hi.astype(F32)).astype(BF16)
    return hi, lo


def _dot(a, b, dims=(((1,), (0,)), ((), ()))):
    return lax.dot_general(a, b, dims, preferred_element_type=F32)


_NT = (((1,), (1,)), ((), ()))
_TN = (((0,), (0,)), ((), ()))


def _dot1(a, b, dims=(((1,), (0,)), ((), ()))):
    return _dot(a.astype(BF16), b.astype(BF16), dims)


def _dot3(a, b, dims=(((1,), (0,)), ((), ()))):
    ah, al = _split(a)
    bh, bl = _split(b)
    return _dot(ah, bh, dims) + (_dot(ah, bl, dims) + _dot(al, bh, dims))


_BMM = (((2,), (1,)), ((0,), (0,)))
_BMM_NT = (((2,), (2,)), ((0,), (0,)))
_BMM_TN = (((1,), (1,)), ((0,), (0,)))
INV_PASSES = 1


def _bdot(a, b, dims=_BMM, passes=1):
    if passes == 1:
        return _dot(a.astype(BF16), b.astype(BF16), dims)
    ah, al = _split(a)
    bh, bl = _split(b)
    return _dot(ah, bh, dims) + (_dot(ah, bl, dims) + _dot(al, bh, dims))


def _split3(x):
    hi = x.astype(BF16)
    r1 = x - hi.astype(F32)
    mid = r1.astype(BF16)
    lo = (r1 - mid.astype(F32)).astype(BF16)
    return hi, mid, lo


def _unit_lower_inverse(a_strict, ii, jj):
    c = a_strict.shape[-1]
    eye = (ii == jj).astype(F32)
    m = None
    s = 1
    k = 0
    while s < c:
        off = jnp.logical_and((ii >> (k + 1)) == (jj >> (k + 1)),
                              jnp.logical_and(((ii >> k) & 1) == 1, ((jj >> k) & 1) == 0))
        a_off = jnp.where(off, a_strict, 0.0)
        if m is None:
            m = eye - a_off
        else:
            m = m - _bdot(m, _bdot(a_off, m, passes=INV_PASSES), passes=INV_PASSES)
        s *= 2
        k += 1
    return m


def _delta_body(q_ref, k_ref, v_ref, z_ref, g_ref, b_ref, s0_ref, gn_ref, o_ref, sf_ref, s_ref,
                *, hb, tl):
    t = pl.program_id(2)
    nt = pl.num_programs(2)
    c = CHUNK
    npc = tl // c

    @pl.when(t == 0)
    def _():
        s_ref[...] = s0_ref[0]

    ii = lax.broadcasted_iota(jnp.int32, (c, c), 0)
    jj = lax.broadcasted_iota(jnp.int32, (c, c), 1)
    tril = ii >= jj
    stril = ii > jj
    tril_b = tril.astype(BF16)
    triu_b = (ii <= jj).astype(BF16)
    eye_b = (ii == jj).astype(BF16)
    gn = gn_ref[...]

    row0 = pl.multiple_of(t * npc, npc)
    g_rows = jnp.concatenate([g_ref[0, hh, pl.ds(row0, npc), :] for hh in range(hb)], axis=0)
    b_rows = jnp.concatenate([b_ref[0, hh, pl.ds(row0, npc), :] for hh in range(hb)], axis=0)
    g3 = _split3(g_rows)
    b3 = _split3(b_rows)
    dcum_rows = _dot(g3[0], triu_b) + (_dot(g3[1], triu_b) + _dot(g3[2], triu_b))
    dcum_cols = _dot(tril_b, g3[0], _NT) + (_dot(tril_b, g3[1], _NT) + _dot(tril_b, g3[2], _NT))
    b_cols = _dot(eye_b, b3[0], _NT) + (_dot(eye_b, b3[1], _NT) + _dot(eye_b, b3[2], _NT))

    ks, qs, decs, rhss, dcs, bcs = [], [], [], [], [], []
    for j in range(npc):
        rs = slice(j * c, (j + 1) * c)
        for hh in range(hb):
            hs = slice(hh * HEAD, (hh + 1) * HEAD)
            gi = hh * npc + j
            k = k_ref[0, rs, hs]
            d_col = dcum_cols[:, gi:gi + 1]
            b_col = b_cols[:, gi:gi + 1]
            diff = d_col - dcum_rows[gi:gi + 1, :]
            decs.append(jnp.where(tril, jnp.exp(jnp.where(tril, diff, 0.0)), 0.0))
            rhss.append(jnp.concatenate([v_ref[0, rs, hs] * b_col,
                                         k * (b_col * jnp.exp(d_col))], axis=1))
            ks.append(k)
            qs.append(q_ref[0, rs, hs])
            dcs.append(d_col)
            bcs.append(b_col)
    kb = jnp.stack(ks)
    qb = jnp.stack(qs)
    dec = jnp.stack(decs)
    d_colb = jnp.stack(dcs)
    d_lastb = d_colb[:, c - 1:c, :]
    a_mat = jnp.stack(bcs) * _bdot(kb, kb, _BMM_NT, passes=INV_PASSES) * jnp.where(stril, dec, 0.0)
    sol = _bdot(_unit_lower_inverse(a_mat, ii, jj), jnp.stack(rhss), passes=INV_PASSES)
    attn = _bdot(qb, kb, _BMM_NT) * dec
    wq = jnp.concatenate([sol[:, :, HEAD:], qb * jnp.exp(d_colb)], axis=1)
    k_dec = kb * jnp.exp(d_lastb - d_colb)
    g_last = jnp.exp(d_lastb)

    s_mat = s_ref[...]
    for j in range(npc):
        ps = slice(j * hb, (j + 1) * hb)
        rs = slice(j * c, (j + 1) * c)
        wq_s = _bdot(wq[ps], s_mat)
        v_new = sol[ps, :, :HEAD] - wq_s[:, :c]
        o = wq_s[:, c:] + _bdot(attn[ps], v_new)
        s_mat = s_mat * g_last[ps] + _bdot(k_dec[ps], v_new, _BMM_TN)
        ms = jnp.mean(o * o, axis=-1, keepdims=True)
        on = o * lax.rsqrt(ms + EPS) * gn
        for hh in range(hb):
            hs = slice(hh * HEAD, (hh + 1) * HEAD)
            z = z_ref[0, rs, hs]
            o_ref[0, rs, hs] = (on[hh] * (z * jax.nn.sigmoid(z))).astype(o_ref.dtype)
    s_ref[...] = s_mat

    @pl.when(t == nt - 1)
    def _():
        sf_ref[0] = s_ref[...]


def delta_prompt(qkv, proj, g, beta, s0, gnorm, n_heads, hb=8, tl=512):
    b, l, cd = qkv.shape
    tl, hb = min(tl, l), min(hb, n_heads)
    nhg = n_heads // hb
    wb = hb * HEAD
    nc = l // CHUNK
    z_off = cd // wb
    return pl.pallas_call(
        functools.partial(_delta_body, hb=hb, tl=tl),
        grid=(b, nhg, l // tl),
        in_specs=[pl.BlockSpec((1, tl, wb), lambda i, h, t: (i, t, h)),
                  pl.BlockSpec((1, tl, wb), lambda i, h, t: (i, t, nhg + h)),
                  pl.BlockSpec((1, tl, wb), lambda i, h, t: (i, t, 2 * nhg + h)),
                  pl.BlockSpec((1, tl, wb), lambda i, h, t: (i, t, z_off + h)),
                  pl.BlockSpec((1, hb, nc, CHUNK), lambda i, h, t: (i, h, 0, 0)),
                  pl.BlockSpec((1, hb, nc, CHUNK), lambda i, h, t: (i, h, 0, 0)),
                  pl.BlockSpec((1, hb, HEAD, HEAD), lambda i, h, t: (i, h, 0, 0)),
                  pl.BlockSpec((1, HEAD), lambda i, h, t: (0, 0))],
        out_specs=[pl.BlockSpec((1, tl, wb), lambda i, h, t: (i, t, h)),
                   pl.BlockSpec((1, hb, HEAD, HEAD), lambda i, h, t: (i, h, 0, 0))],
        out_shape=[jax.ShapeDtypeStruct((b, l, n_heads * HEAD), BF16),
                   jax.ShapeDtypeStruct((b, n_heads, HEAD, HEAD), F32)],
        scratch_shapes=[pltpu.VMEM((hb, HEAD, HEAD), F32)],
        compiler_params=_cparams(("parallel", "parallel", "arbitrary")),
        name="delta_prompt",
    )(qkv, qkv, qkv, proj, g, beta, s0, gnorm.reshape(1, HEAD))


def _delta_dec_body(q_ref, k_ref, v_ref, z_ref, g_ref, b_ref, s0_ref, gn_ref, o_ref, sf_ref, *, nh):
    ii = lax.broadcasted_iota(jnp.int32, (HEAD, HEAD), 0)
    jj = lax.broadcasted_iota(jnp.int32, (HEAD, HEAD), 1)
    eye = ii == jj
    gn = gn_ref[...]

    def col_of(row):
        return jnp.sum(jnp.where(eye, jnp.broadcast_to(row, (HEAD, HEAD)), 0.0), axis=1, keepdims=True)

    def head(h, carry):
        q = q_ref[0, pl.ds(h, 1), :]
        k = k_ref[0, pl.ds(h, 1), :]
        v = v_ref[0, pl.ds(h, 1), :]
        z = z_ref[0, pl.ds(h, 1), :]
        eg = jnp.exp(g_ref[0, pl.ds(h, 1), :])
        beta = b_ref[0, pl.ds(h, 1), :]
        s_mat = s0_ref[0, h]
        k_col = col_of(k)
        ks = jnp.sum(k_col * s_mat, axis=0, keepdims=True)
        qs = jnp.sum(col_of(q) * s_mat, axis=0, keepdims=True)
        v_new = v * beta - (beta * eg) * ks
        qk = jnp.sum(q * k, axis=-1, keepdims=True)
        o = eg * qs + qk * v_new
        sf_ref[0, h] = s_mat * eg + k_col * v_new
        ms = jnp.mean(o * o, axis=-1, keepdims=True)
        on = o * lax.rsqrt(ms + EPS) * gn
        o_ref[0, pl.ds(h, 1), :] = (on * (z * jax.nn.sigmoid(z))).astype(o_ref.dtype)
        return carry

    lax.fori_loop(0, nh, head, 0)


def delta_decode(q, k, v, z, g_rep, b_rep, s0, gnorm):
    b, nh, _ = q.shape
    vec = pl.BlockSpec((1, nh, HEAD), lambda i: (i, 0, 0))
    st = pl.BlockSpec((1, nh, HEAD, HEAD), lambda i: (i, 0, 0, 0))
    return pl.pallas_call(
        functools.partial(_delta_dec_body, nh=nh),
        grid=(b,),
        in_specs=[vec, vec, vec, vec, vec, vec, st, pl.BlockSpec((1, HEAD), lambda i: (0, 0))],
        out_specs=[vec, st],
        out_shape=[jax.ShapeDtypeStruct((b, nh, HEAD), F32),
                   jax.ShapeDtypeStruct((b, nh, HEAD, HEAD), F32)],
        compiler_params=_cparams(("parallel",)),
        name="delta_decode",
    )(q, k, v, z, g_rep, b_rep, s0, gnorm.reshape(1, HEAD))


def _cumsum_body(*refs, pps):
    x_refs = refs[1:1 + pps]
    o_ref, tot_ref, carry_ref = refs[1 + pps:]
    p = pl.program_id(1)
    npg = pl.num_programs(1)

    @pl.when(p == 0)
    def _():
        carry_ref[...] = jnp.zeros_like(carry_ref)

    n = x_refs[0].shape[1]
    tri = (lax.broadcasted_iota(jnp.int32, (n, n), 0)
           >= lax.broadcasted_iota(jnp.int32, (n, n), 1)).astype(BF16)
    carry = carry_ref[...]
    for r, x_ref in enumerate(x_refs):
        hi, mid, lo = _split3(x_ref[0])
        out = (_dot(tri, hi) + (_dot(tri, mid) + _dot(tri, lo))) + carry
        o_ref[0, r * n:(r + 1) * n, :] = out
        carry = out[n - 1:n, :]
    carry_ref[...] = carry

    @pl.when(p == npg - 1)
    def _():
        tot_ref[0] = carry


def paged_cumsum(pages, table):
    b, npg = table.shape
    _, pg, h = pages.shape
    pps = next(c for c in (8, 4, 2, 1) if npg % c == 0)
    page_spec = lambda r: pl.BlockSpec((1, pg, h), lambda i, p, pt: (pt[i, p * pps + r], 0, 0))
    return pl.pallas_call(
        functools.partial(_cumsum_body, pps=pps),
        grid_spec=pltpu.PrefetchScalarGridSpec(
            num_scalar_prefetch=1,
            grid=(b, npg // pps),
            in_specs=[page_spec(r) for r in range(pps)],
            out_specs=[pl.BlockSpec((1, pps * pg, h), lambda i, p, pt: (i, p, 0)),
                       pl.BlockSpec((1, 1, h), lambda i, p, pt: (i, 0, 0))],
            scratch_shapes=[pltpu.VMEM((1, h), F32)]),
        out_shape=[jax.ShapeDtypeStruct((b, npg * pg, h), F32),
                   jax.ShapeDtypeStruct((b, 1, h), F32)],
        compiler_params=_cparams(("parallel", "arbitrary")),
        name="paged_cumsum",
    )(table, *([pages] * pps))


def _lane_pack3(x, nh):
    pieces = _split3(x)
    row = lax.broadcasted_iota(jnp.int32, (nh, HEAD), 0)
    lane = lax.broadcasted_iota(jnp.int32, (nh, HEAD), 1)
    out = None
    for gi, pc in enumerate(pieces):
        part = _dot(pc, (lane == gi * nh + row).astype(BF16))
        out = part if out is None else out + part
    return out


def _bias_cols_body(cum_ref, qx_ref, kx_ref, *, nh):
    h = pl.program_id(1)
    packed = _lane_pack3(cum_ref[0], nh).astype(BF16)
    r = lax.broadcasted_iota(jnp.int32, (HEAD, HEAD), 0)
    j = lax.broadcasted_iota(jnp.int32, (HEAD, HEAD), 1)
    piece = lax.div(r, nh)
    mine = jnp.logical_and(lax.rem(r, nh) == h, piece < 3)
    to_k = jnp.logical_and(mine, j == piece).astype(BF16)
    to_q = jnp.logical_and(mine, j == piece + 3).astype(BF16)
    lane = lax.broadcasted_iota(jnp.int32, (1, HEAD), 1)
    kx_ref[0, 0] = (jnp.logical_and(lane >= 3, lane < 6).astype(F32) - _dot(packed, to_k)).astype(BF16)
    qx_ref[0, 0] = ((lane < 3).astype(F32) + _dot(packed, to_q)).astype(BF16)


def fox_bias_cols(cum, n_heads):
    b, l, _ = cum.shape
    out = pl.BlockSpec((1, 1, l, HEAD), lambda i, h: (i, h, 0, 0))
    return pl.pallas_call(
        functools.partial(_bias_cols_body, nh=n_heads),
        grid=(b, n_heads),
        in_specs=[pl.BlockSpec((1, l, n_heads), lambda i, h: (i, 0, 0))],
        out_specs=[out, out],
        out_shape=[jax.ShapeDtypeStruct((b, n_heads, l, HEAD), BF16)] * 2,
        compiler_params=_cparams(("parallel", "parallel")),
        name="fox_bias_cols",
    )(cum)


def _flash_body(q_ref, k_ref, v_ref, qx_ref, kx_ref, o_ref, *, tq):
    nq = q_ref.shape[1] // tq
    causal = (lax.broadcasted_iota(jnp.int32, (tq, tq), 0)
              >= lax.broadcasted_iota(jnp.int32, (tq, tq), 1))
    for qi in range(nq):
        rq = slice(qi * tq, (qi + 1) * tq)
        qa = jnp.concatenate([q_ref[0, rq, :], qx_ref[0, 0, rq, :]], axis=1)
        m = jnp.full((tq, 1), NEG, F32)
        lsum = jnp.zeros((tq, 1), F32)
        acc = jnp.zeros((tq, HEAD), F32)
        for ki in range(qi + 1):
            rk = slice(ki * tq, (ki + 1) * tq)
            ka = jnp.concatenate([k_ref[0, rk, :], kx_ref[0, 0, rk, :]], axis=1)
            s = _dot(qa, ka, _NT)
            if ki == qi:
                s = jnp.where(causal, s, NEG)
            m_new = jnp.maximum(m, jnp.max(s, axis=1, keepdims=True))
            alpha = jnp.exp(m - m_new)
            p = jnp.exp(s - m_new)
            lsum = alpha * lsum + jnp.sum(p, axis=1, keepdims=True)
            acc = alpha * acc + _dot(p.astype(BF16), v_ref[0, rk, :])
            m = m_new
        o_ref[0, rq, :] = (acc / lsum).astype(o_ref.dtype)


def fox_prompt(q, k, v, qx, kx, n_heads, tq=512):
    b, l, _ = q.shape
    tq = min(tq, l)
    tok = pl.BlockSpec((1, l, HEAD), lambda i, h: (i, 0, h))
    col = pl.BlockSpec((1, 1, l, HEAD), lambda i, h: (i, h, 0, 0))
    return pl.pallas_call(
        functools.partial(_flash_body, tq=tq),
        grid=(b, n_heads),
        in_specs=[tok, tok, tok, col, col],
        out_specs=tok,
        out_shape=jax.ShapeDtypeStruct(q.shape, BF16),
        compiler_params=_cparams(("parallel", "parallel")),
        name="fox_prompt",
    )(q, k, v, qx, kx)


def _fox_dec_body(*refs, scale, nh, pps):
    q_ref = refs[1]
    kc_refs = refs[2:2 + pps]
    vc_refs = refs[2 + pps:2 + 2 * pps]
    ck_ref, tot_ref, lf_ref, kn_ref, vn_ref, o_ref, m_ref, l_ref, acc_ref = refs[2 + 2 * pps:]
    npos = pps * PAGE
    p = pl.program_id(1)
    npg = pl.num_programs(1)

    @pl.when(p == 0)
    def _():
        m_ref[...] = jnp.full_like(m_ref, NEG)
        l_ref[...] = jnp.zeros_like(l_ref)
        acc_ref[...] = jnp.zeros_like(acc_ref)

    q = q_ref[0]
    ones = jnp.ones((HEAD, HEAD), BF16)
    sub = lax.broadcasted_iota(jnp.int32, (nh, HEAD), 0)
    lane = lax.broadcasted_iota(jnp.int32, (nh, HEAD), 1)
    own = jnp.logical_and(lax.rem(lane, nh) == sub, lane < 3 * nh)

    def head_rows(packed):
        r = packed.shape[0]
        z = jnp.where(own[None], jnp.broadcast_to(packed[:, None, :], (r, nh, HEAD)), 0.0)
        return _dot(z.reshape(r * nh, HEAD).astype(BF16), ones).reshape(r, nh, HEAD)

    cq = head_rows(_lane_pack3(jnp.broadcast_to(tot_ref[0] + lf_ref[0], (8, nh)), nh))[0]
    ck = head_rows(_lane_pack3(ck_ref[0], nh))
    kc = jnp.concatenate([r[...].reshape(PAGE, nh, HEAD) for r in kc_refs], axis=0)
    vc = jnp.concatenate([r[...].reshape(PAGE, nh, HEAD) for r in vc_refs], axis=0)
    qk = _dot((kc * q[None]).reshape(npos * nh, HEAD).astype(BF16), ones).reshape(npos, nh, HEAD)
    s = qk * scale + (cq[None] - ck)
    m_prev = m_ref[...]
    m_new = jnp.maximum(m_prev, jnp.max(s, axis=0))
    alpha = jnp.exp(m_prev - m_new)
    pe = jnp.exp(s - m_new[None])
    l_new = alpha * l_ref[...] + jnp.sum(pe, axis=0)
    acc_new = alpha * acc_ref[...] + jnp.sum(pe * vc, axis=0)
    m_ref[...] = m_new
    l_ref[...] = l_new
    acc_ref[...] = acc_new

    @pl.when(p == npg - 1)
    def _():
        s_n = jnp.sum(kn_ref[0] * q, axis=-1, keepdims=True) * scale + (cq - cq)
        m_fin = jnp.maximum(m_new, s_n)
        a_fin = jnp.exp(m_new - m_fin)
        p_n = jnp.exp(s_n - m_fin)
        l_fin = a_fin * l_new + p_n
        o_ref[0] = (a_fin * acc_new + p_n * vn_ref[0]) / l_fin


def fox_decode(q, k_cache, v_cache, table, ck_past, total, logf_new, k_new, v_new, n_heads):
    b, npg = table.shape
    rows = PAGE * n_heads
    pps = 2 if npg % 2 == 0 else 1
    vec = pl.BlockSpec((1, n_heads, HEAD), lambda i, p, pt: (i, 0, 0))
    row = pl.BlockSpec((1, 1, n_heads), lambda i, p, pt: (i, 0, 0))
    pages = [pl.BlockSpec((rows, HEAD), lambda i, p, pt, r=r: (pt[i, p * pps + r], 0))
             for r in range(pps)]
    return pl.pallas_call(
        functools.partial(_fox_dec_body, scale=HEAD ** -0.5, nh=n_heads, pps=pps),
        grid_spec=pltpu.PrefetchScalarGridSpec(
            num_scalar_prefetch=1,
            grid=(b, npg // pps),
            in_specs=[vec] + pages + pages
            + [pl.BlockSpec((1, pps * PAGE, n_heads), lambda i, p, pt: (i, p, 0)),
               row, row, vec, vec],
            out_specs=vec,
            scratch_shapes=[pltpu.VMEM((n_heads, HEAD), F32), pltpu.VMEM((n_heads, HEAD), F32),
                            pltpu.VMEM((n_heads, HEAD), F32)]),
        out_shape=jax.ShapeDtypeStruct((b, n_heads, HEAD), F32),
        compiler_params=_cparams(("parallel", "arbitrary")),
        name="fox_decode",
    )(table, q, *([k_cache] * pps), *([v_cache] * pps), ck_past, total, logf_new, k_new, v_new)


class _Weights:
    def __init__(self, big, a_w_in, w_f, n_main):
        self.f32 = big
        self.bf16 = {}
        self.n_main = n_main
        n_ab = a_w_in.shape[-1] - n_main
        self.w_ab = jnp.pad(a_w_in[:, :, n_main:], ((0, 0), (0, 0), (0, HEAD - n_ab))).astype(BF16)
        self.w_f = jnp.pad(w_f, ((0, 0), (0, HEAD - w_f.shape[1]))).astype(BF16)

    def mm(self, name, li, a, **kw):
        n = self.n_main if name == "w_in" else None
        if (name, li) in self.bf16:
            return matmul(a, self.bf16[name, li], **kw)
        out = matmul(a, self.f32[name], layer=li, n=n, emit_w=True, **kw)
        self.bf16[name, li] = out[-1]
        return out[0] if len(out) == 2 else out[:-1]


def _trunk(x, conv_state, delta_state, past, wts, prm):
    b, l, d = x.shape
    m = b * l
    n_a = prm["a_log"].shape[0]
    n_h = prm["a_log"].shape[1]
    qk_dim = n_h * HEAD
    conv_dim = prm["a_w_conv"].shape[-1]
    decode = past is not None
    h = x.reshape(m, d)
    pend = None

    def norm(g):
        nonlocal h, pend
        if pend is None:
            return rmsnorm(h, g)
        xn, h = rmsnorm(h, g, add=pend, emit_sum=True)
        pend = None
        return xn

    def mlp(li):
        hid = wts.mm("w_up", li, norm(prm["norm_mlp_g"][li]), out_dtypes=(BF16,), relu2=True)
        return wts.mm("w_down", li, hid)

    new_conv, new_delta = [], []
    for li in range(n_a):
        xn = norm(prm["norm_mix_g"][li])
        proj = wts.mm("w_in", li, xn)
        ab = matmul(xn, wts.w_ab, layer=li)
        gb = gdn_gates(ab, prm["a_log"][li], prm["a_dt_bias"][li])
        g, beta = gb[:, :n_h], gb[:, n_h:2 * n_h]
        if decode:
            buf = conv_state[li]
            qkv = conv_decode(proj, buf, prm["a_w_conv"][li], qk_dim)
            new_conv.append(jnp.concatenate([buf[:, 1:], proj[:, None, :conv_dim]], axis=1))
            hv = lambda t: t.reshape(b, n_h, HEAD)
            rep = lambda t: jnp.broadcast_to(t[:, :, None], (b, n_h, HEAD))
            o, s_new = delta_decode(hv(qkv[:, :qk_dim]), hv(qkv[:, qk_dim:2 * qk_dim]),
                                    hv(qkv[:, 2 * qk_dim:]), hv(proj[:, conv_dim:]),
                                    rep(g), rep(beta), delta_state[li], prm["a_o_norm_g"][li])
            o = o.reshape(m, n_h * HEAD).astype(BF16)
        else:
            proj3 = proj.reshape(b, l, -1)
            buf8 = jnp.pad(conv_state[li], ((0, 0), (8 - (CONV_W - 1), 0), (0, 0)))
            qkv = conv_prompt(proj3, buf8, prm["a_w_conv"][li], qk_dim)
            new_conv.append(proj3[:, l - (CONV_W - 1):, :conv_dim])
            chunks = lambda t: t.reshape(b, l // CHUNK, CHUNK, n_h).transpose(0, 3, 1, 2)
            o, s_new = delta_prompt(qkv, proj3, chunks(g), chunks(beta), delta_state[li],
                                    prm["a_o_norm_g"][li], n_h)
            o = o.reshape(m, n_h * HEAD)
        new_delta.append(s_new)
        h = wts.mm("w_out", li, o, res=h)
        pend = mlp(li)

    hn = norm(prm["kv_norm_g"])
    nb = wts.f32["w_k"].shape[1]
    n_hb = nb // HEAD
    k_new, k_bf = wts.mm("w_k", None, hn, out_dtypes=(F32, BF16))
    v_new, v_bf = wts.mm("w_v", None, hn, out_dtypes=(F32, BF16))
    logf = bias_log_sigmoid(matmul(hn, wts.w_f), prm["b_f"])[:, :n_hb]
    if decode:
        k_cache, v_cache, logf_cache, table = past
        ck_past, total = paged_cumsum(logf_cache, table)
        k_cache = k_cache.reshape(-1, HEAD)
        v_cache = v_cache.reshape(-1, HEAD)
    else:
        npg = l // PAGE
        table = jnp.arange(b * npg, dtype=jnp.int32).reshape(b, npg)
        cum, _ = paged_cumsum(logf.reshape(b * npg, PAGE, n_hb), table)
        qx, kx = fox_bias_cols(cum, n_hb)
        k_bf = k_bf.reshape(b, l, nb)
        v_bf = v_bf.reshape(b, l, nb)
    for j in range(prm["b_w_q"].shape[0]):
        li = n_a + j
        xn = norm(prm["norm_mix_g"][li])
        if decode:
            q = wts.mm("w_q", j, xn)
            hv = lambda t: t.reshape(b, n_hb, HEAD)
            o = fox_decode(hv(q), k_cache, v_cache, table, ck_past, total,
                           logf.reshape(b, 1, n_hb), hv(k_new), hv(v_new), n_hb).astype(BF16)
        else:
            q = wts.mm("w_q", j, xn, out_dtypes=(BF16,), scale=HEAD ** -0.5)
            o = fox_prompt(q.reshape(b, l, nb), k_bf, v_bf, qx, kx, n_hb)
        h = wts.mm("w_o", j, o.reshape(m, nb), res=h)
        pend = mlp(li)
    y = rmsnorm(h, prm["final_norm_g"], out_dtype=F32, add=pend)
    return (y.reshape(b, l, d), jnp.stack(new_delta), jnp.stack(new_conv),
            k_new.reshape(b, l, n_hb, HEAD), v_new.reshape(b, l, n_hb, HEAD),
            logf.reshape(b, l, n_hb))


def kernel(x_prompt, x_sample, cache_k, cache_v, cache_logf, state_delta, state_conv, page_table,
           norm_mix_g, norm_mlp_g, w_up, w_down, a_w_in, a_w_conv, a_log, a_dt_bias, a_o_norm_g,
           a_w_out, kv_norm_g, w_k, w_v, w_f, b_f, b_w_q, b_w_o, final_norm_g):
    conv_dim = a_w_conv.shape[-1]
    v_dim = a_w_out.shape[1]
    wts = _Weights(dict(w_up=w_up, w_down=w_down, w_in=a_w_in, w_out=a_w_out, w_k=w_k, w_v=w_v,
                        w_q=b_w_q, w_o=b_w_o), a_w_in, w_f, conv_dim + v_dim)
    prm = dict(norm_mix_g=norm_mix_g, norm_mlp_g=norm_mlp_g, a_w_conv=a_w_conv, a_log=a_log,
               a_dt_bias=a_dt_bias, a_o_norm_g=a_o_norm_g, kv_norm_g=kv_norm_g, b_f=b_f,
               b_w_q=b_w_q, final_norm_g=final_norm_g)
    n_a = a_log.shape[0]
    n_p = x_prompt.shape[0]
    n_h = a_log.shape[1]
    zero_conv = jnp.zeros((n_a, n_p, CONV_W - 1, conv_dim), state_conv.dtype)
    zero_delta = jnp.zeros((n_a, n_p, n_h, HEAD, HEAD), state_delta.dtype)
    y_s, delta_s, conv_s, k_s, v_s, logf_s = _trunk(
        x_sample, state_conv, state_delta, (cache_k, cache_v, cache_logf, page_table), wts, prm)
    y_p, delta_p, conv_p, k_p, v_p, logf_p = _trunk(x_prompt, zero_conv, zero_delta, None, wts, prm)
    return (y_p, y_s, delta_p, conv_p, k_p, v_p, logf_p, delta_s, conv_s, k_s, v_s, logf_s)
```

```python
import functools

import jax
import jax.numpy as jnp
from jax import lax
from jax.experimental import pallas as pl
from jax.experimental.pallas import tpu as pltpu

F32 = jnp.float32
BF16 = jnp.bfloat16

EPS = 1e-6
HEAD = 128
CHUNK = 64
CONV_W = 4
PAGE = 128
NEG = -1e30
V7X_VMEM_LIMIT = 56 * 1024 * 1024


def _cparams(sem):
    return pltpu.CompilerParams(dimension_semantics=sem, vmem_limit_bytes=V7X_VMEM_LIMIT)


def _rms_body(*refs, has_add, emit_sum):
    x = refs[0][...]
    if has_add:
        x = x + refs[1][...]
    g_ref = refs[1 + has_add]
    outs = refs[2 + has_add:]
    ms = jnp.mean(x * x, axis=-1, keepdims=True)
    outs[0][...] = (x * lax.rsqrt(ms + EPS) * g_ref[...]).astype(outs[0].dtype)
    if emit_sum:
        outs[1][...] = x


def rmsnorm(x, g, out_dtype=BF16, add=None, emit_sum=False):
    m, d = x.shape
    tm = min(m, 256 if add is not None else 512)
    row = pl.BlockSpec((tm, d), lambda i: (i, 0))
    ins = [x] + ([add] if add is not None else [])
    out = pl.pallas_call(
        functools.partial(_rms_body, has_add=add is not None, emit_sum=emit_sum),
        grid=(m // tm,),
        in_specs=[row] * len(ins) + [pl.BlockSpec((1, d), lambda i: (0, 0))],
        out_specs=[row] * (1 + emit_sum),
        out_shape=[jax.ShapeDtypeStruct((m, d), out_dtype)]
        + ([jax.ShapeDtypeStruct((m, d), F32)] if emit_sum else []),
        compiler_params=_cparams(("parallel",)),
        name="rmsnorm",
    )(*ins, g.reshape(1, d))
    return tuple(out) if emit_sum else out[0]


def _mm_body(*refs, nk, relu2, scale, has_res, n_out, emit_w):
    a_ref, w_ref = refs[0], refs[1]
    pos = 2
    r_ref = None
    if has_res:
        r_ref = refs[pos]
        pos += 1
    o_refs = refs[pos:pos + n_out]
    pos += n_out
    w = w_ref[...]
    if emit_w:
        w = w.astype(BF16)
        refs[pos][...] = w
        pos += 1
    scr = refs[pos:]

    def finish(acc):
        if relu2:
            acc = jnp.maximum(acc, 0.0)
            acc = acc * acc
        if scale is not None:
            acc = acc * scale
        if has_res:
            acc = r_ref[...] + acc
        for o_ref in o_refs:
            o_ref[...] = acc.astype(o_ref.dtype)

    part = jnp.dot(a_ref[...], w, preferred_element_type=F32)
    if nk == 1:
        finish(part)
    else:
        acc_ref = scr[0]
        k = pl.program_id(2)

        @pl.when(k == 0)
        def _():
            acc_ref[...] = part

        @pl.when(jnp.logical_and(k > 0, k < nk - 1))
        def _():
            acc_ref[...] += part

        @pl.when(k == nk - 1)
        def _():
            finish(acc_ref[...] + part)


def matmul(a, w, out_dtypes=(F32,), res=None, relu2=False, scale=None, layer=None, n=None,
           emit_w=False, tm=1024, tn=1024, tk=4096):
    m, kd = a.shape
    n = w.shape[-1] if n is None else n
    if kd > tk and (res is not None or emit_w):
        tk //= 2
    tm, tn, tk = min(tm, m), min(tn, n), min(tk, kd)
    nk = kd // tk
    assert not emit_w or m == tm
    if layer is None:
        w_spec = pl.BlockSpec((tk, tn), lambda i, j, k: (k, j))
    else:
        w_spec = pl.BlockSpec((None, tk, tn), lambda i, j, k: (layer, k, j))
    in_specs = [pl.BlockSpec((tm, tk), lambda i, j, k: (i, k)), w_spec]
    args = [a, w]
    if res is not None:
        in_specs.append(pl.BlockSpec((tm, tn), lambda i, j, k: (i, j)))
        args.append(res)
    out_specs = [pl.BlockSpec((tm, tn), lambda i, j, k: (i, j)) for _ in out_dtypes]
    out_shape = [jax.ShapeDtypeStruct((m, n), dt) for dt in out_dtypes]
    if emit_w:
        out_specs.append(pl.BlockSpec((tk, tn), lambda i, j, k: (k, j)))
        out_shape.append(jax.ShapeDtypeStruct((kd, n), BF16))
    out = pl.pallas_call(
        functools.partial(_mm_body, nk=nk, relu2=relu2, scale=scale, has_res=res is not None,
                          n_out=len(out_dtypes), emit_w=emit_w),
        grid=(m // tm, n // tn, nk),
        in_specs=in_specs,
        out_specs=out_specs,
        out_shape=out_shape,
        scratch_shapes=[pltpu.VMEM((tm, tn), F32)] if nk > 1 else [],
        compiler_params=_cparams(("parallel", "parallel", "arbitrary")),
        name="matmul",
    )(*args)
    return out[0] if len(out) == 1 else tuple(out)


def _softplus(x):
    return jnp.maximum(x, 0.0) + jnp.log1p(jnp.exp(-jnp.abs(x)))


def _gates_body(x_ref, alog_ref, dt_ref, o_ref, *, nh):
    x = x_ref[...]
    lane = lax.broadcasted_iota(jnp.int32, x.shape, 1)
    g = -jnp.exp(alog_ref[...]) * _softplus(x + dt_ref[...])
    beta = jax.nn.sigmoid(x)
    o_ref[...] = jnp.where(lane < nh, g, beta)


def gdn_gates(ab, a_log, dt_bias):
    m, w = ab.shape
    nh = a_log.shape[0]
    tm = min(m, 1024)
    pad = lambda v: jnp.pad(v.astype(F32), (0, w - nh)).reshape(1, w)
    return pl.pallas_call(
        functools.partial(_gates_body, nh=nh),
        grid=(m // tm,),
        in_specs=[pl.BlockSpec((tm, w), lambda i: (i, 0)),
                  pl.BlockSpec((1, w), lambda i: (0, 0)),
                  pl.BlockSpec((1, w), lambda i: (0, 0))],
        out_specs=pl.BlockSpec((tm, w), lambda i: (i, 0)),
        out_shape=jax.ShapeDtypeStruct((m, w), F32),
        compiler_params=_cparams(("parallel",)),
        name="gdn_gates",
    )(ab, pad(a_log), pad(dt_bias))


def _logsig_body(x_ref, b_ref, o_ref):
    y = x_ref[...] + b_ref[...]
    o_ref[...] = -_softplus(-y)


def bias_log_sigmoid(x, b):
    m, w = x.shape
    tm = min(m, 1024)
    return pl.pallas_call(
        _logsig_body,
        grid=(m // tm,),
        in_specs=[pl.BlockSpec((tm, w), lambda i: (i, 0)),
                  pl.BlockSpec((1, w), lambda i: (0, 0))],
        out_specs=pl.BlockSpec((tm, w), lambda i: (i, 0)),
        out_shape=jax.ShapeDtypeStruct((m, w), F32),
        compiler_params=_cparams(("parallel",)),
        name="bias_log_sigmoid",
    )(x, jnp.pad(b.astype(F32), (0, w - b.shape[0])).reshape(1, w))


def _l2_heads(y, o_ref, idx, qscale):
    for j in range(y.shape[-1] // HEAD):
        ys = y[..., j * HEAD:(j + 1) * HEAD]
        ss = jnp.sum(ys * ys, axis=-1, keepdims=True)
        o_ref[idx + (slice(j * HEAD, (j + 1) * HEAD),)] = ys * lax.rsqrt(ss + EPS) * qscale


def _in_conv_body(a_ref, w_ref, wc_ref, buf_ref, o_ref, tail_ref, ext_ref, halo_ref,
                  *, tm, n_q, n_conv, tiles_per_seq):
    i = pl.program_id(0)
    j = pl.program_id(1)
    jc = jnp.minimum(j, n_conv)
    first = lax.rem(i, tiles_per_seq) == 0
    nsub, _, sub = ext_ref.shape[0], ext_ref.shape[1], ext_ref.shape[2]

    @pl.when(first)
    def _():
        for s in range(nsub):
            ext_ref[s, 0:8, :] = buf_ref[0, :, s * sub:(s + 1) * sub]

    @pl.when(jnp.logical_not(first))
    def _():
        for s in range(nsub):
            ext_ref[s, 0:8, :] = halo_ref[jc, :, s * sub:(s + 1) * sub]

    is_conv = j < n_conv
    is_qk = j < 2 * n_q
    qscale = jnp.where(j < n_q, HEAD ** -0.5, 1.0).astype(F32)

    def finish(s, acc):
        cs = slice(s * sub, (s + 1) * sub)
        ext_ref[s, 8:8 + tm, :] = acc
        wc = wc_ref[:, cs]
        cv = ext_ref[s, 5:5 + tm, :] * wc[0:1]
        cv = cv + ext_ref[s, 6:6 + tm, :] * wc[1:2]
        cv = cv + ext_ref[s, 7:7 + tm, :] * wc[2:3]
        cv = cv + acc * wc[3:4]
        y = cv * jax.nn.sigmoid(cv)
        tail = acc[tm - 8:tm]
        halo_ref[jc, :, cs] = tail
        tail_ref[0, :, cs] = tail
        for hh in range(sub // HEAD):
            hs = slice(hh * HEAD, (hh + 1) * HEAD)
            ys = y[:, hs]
            ss = jnp.sum(ys * ys, axis=-1, keepdims=True)
            inv = jnp.where(is_qk, lax.rsqrt(ss + EPS), 1.0)
            o_ref[:, s * sub + hh * HEAD:s * sub + (hh + 1) * HEAD] = jnp.where(
                is_conv, ys * inv * qscale, acc[:, hs])

    pending = None
    for s in range(nsub):
        acc = jnp.dot(a_ref[...], w_ref[:, s * sub:(s + 1) * sub], preferred_element_type=F32)
        if pending is not None:
            finish(*pending)
        pending = (s, acc)
    finish(*pending)


def in_proj_conv(xn, w, buf8, w_conv, qk_dim, seq_len, n_main, tm=1024, tn=1024):
    m, kd = xn.shape
    cd = w_conv.shape[1]
    tm, tn = min(tm, seq_len), min(tn, qk_dim)
    sub = min(tn, 256)
    tps = seq_len // tm
    n_conv = cd // tn
    cj = lambda j: jnp.minimum(j, n_conv - 1)
    out, tail = pl.pallas_call(
        functools.partial(_in_conv_body, tm=tm, n_q=qk_dim // tn, n_conv=n_conv,
                          tiles_per_seq=tps),
        grid=(m // tm, n_main // tn),
        in_specs=[pl.BlockSpec((tm, kd), lambda i, j: (i, 0)),
                  pl.BlockSpec((kd, tn), lambda i, j: (0, j)),
                  pl.BlockSpec((CONV_W, tn), lambda i, j: (0, cj(j))),
                  pl.BlockSpec((1, 8, tn), lambda i, j: (i // tps, 0, cj(j)))],
        out_specs=[pl.BlockSpec((tm, tn), lambda i, j: (i, j)),
                   pl.BlockSpec((1, 8, tn), lambda i, j: (i, 0, jnp.minimum(j, n_conv)))],
        out_shape=[jax.ShapeDtypeStruct((m, n_main), F32),
                   jax.ShapeDtypeStruct((m // tm, 8, cd + tn), F32)],
        scratch_shapes=[pltpu.VMEM((tn // sub, tm + 8, sub), F32),
                        pltpu.VMEM((n_conv + 1, 8, tn), F32)],
        compiler_params=_cparams(("arbitrary", "arbitrary")),
        name="in_proj_conv",
    )(xn, w, w_conv, buf8)
    return out, tail.reshape(m // seq_len, tps, 8, cd + tn)[:, tps - 1, :, :cd]


def _conv_dec_body(u_ref, buf_ref, w_ref, o_ref, *, n_q_tiles):
    c = pl.program_id(0)
    w = w_ref[...]
    acc = buf_ref[:, 0, :] * w[0:1]
    acc = acc + buf_ref[:, 1, :] * w[1:2]
    acc = acc + buf_ref[:, 2, :] * w[2:3]
    acc = acc + u_ref[...] * w[3:4]
    y = acc * jax.nn.sigmoid(acc)

    @pl.when(c < 2 * n_q_tiles)
    def _():
        qscale = jnp.where(c < n_q_tiles, HEAD ** -0.5, 1.0).astype(F32)
        _l2_heads(y, o_ref, (slice(None),), qscale)

    @pl.when(c >= 2 * n_q_tiles)
    def _():
        o_ref[...] = y


def conv_decode(proj, buf, w_conv, qk_dim, tc=512):
    b = proj.shape[0]
    cd = w_conv.shape[1]
    tc = min(tc, qk_dim)
    return pl.pallas_call(
        functools.partial(_conv_dec_body, n_q_tiles=qk_dim // tc),
        grid=(cd // tc,),
        in_specs=[pl.BlockSpec((b, tc), lambda c: (0, c)),
                  pl.BlockSpec((b, CONV_W - 1, tc), lambda c: (0, 0, c)),
                  pl.BlockSpec((CONV_W, tc), lambda c: (0, c))],
        out_specs=pl.BlockSpec((b, tc), lambda c: (0, c)),
        out_shape=jax.ShapeDtypeStruct((b, cd), F32),
        compiler_params=_cparams(("parallel",)),
        name="conv_decode",
    )(proj, buf, w_conv)


def _split(x):
    hi = x.astype(BF16)
    lo = (x - hi.astype(F32)).astype(BF16)
    return hi, lo


def _dot(a, b, dims=(((1,), (0,)), ((), ()))):
    return lax.dot_general(a, b, dims, preferred_element_type=F32)


_NT = (((1,), (1,)), ((), ()))
_TN = (((0,), (0,)), ((), ()))


def _dot1(a, b, dims=(((1,), (0,)), ((), ()))):
    return _dot(a.astype(BF16), b.astype(BF16), dims)


def _dot3(a, b, dims=(((1,), (0,)), ((), ()))):
    ah, al = _split(a)
    bh, bl = _split(b)
    return _dot(ah, bh, dims) + (_dot(ah, bl, dims) + _dot(al, bh, dims))


_BMM = (((2,), (1,)), ((0,), (0,)))
_BMM_NT = (((2,), (2,)), ((0,), (0,)))
_BMM_TN = (((1,), (1,)), ((0,), (0,)))
INV_PASSES = 1


def _bdot(a, b, dims=_BMM, passes=1):
    if passes == 1:
        return _dot(a.astype(BF16), b.astype(BF16), dims)
    ah, al = _split(a)
    bh, bl = _split(b)
    return _dot(ah, bh, dims) + (_dot(ah, bl, dims) + _dot(al, bh, dims))


def _split3(x):
    hi = x.astype(BF16)
    r1 = x - hi.astype(F32)
    mid = r1.astype(BF16)
    lo = (r1 - mid.astype(F32)).astype(BF16)
    return hi, mid, lo


def _unit_lower_inverse(a_strict, ii, jj):
    c = a_strict.shape[-1]
    eye = (ii == jj).astype(F32)
    m = None
    s = 1
    k = 0
    while s < c:
        off = jnp.logical_and((ii >> (k + 1)) == (jj >> (k + 1)),
                              jnp.logical_and(((ii >> k) & 1) == 1, ((jj >> k) & 1) == 0))
        a_off = jnp.where(off, a_strict, 0.0)
        if m is None:
            m = eye - a_off
        else:
            m = m - _bdot(m, _bdot(a_off, m, passes=INV_PASSES), passes=INV_PASSES)
        s *= 2
        k += 1
    return m


def _delta_body(q_ref, k_ref, v_ref, z_ref, g_ref, b_ref, s0_ref, gn_ref, o_ref, sf_ref, s_ref,
                *, hb, tl):
    t = pl.program_id(2)
    nt = pl.num_programs(2)
    c = CHUNK
    npc = tl // c

    @pl.when(t == 0)
    def _():
        s_ref[...] = s0_ref[0]

    ii = lax.broadcasted_iota(jnp.int32, (c, c), 0)
    jj = lax.broadcasted_iota(jnp.int32, (c, c), 1)
    tril = ii >= jj
    stril = ii > jj
    tril_b = tril.astype(BF16)
    triu_b = (ii <= jj).astype(BF16)
    eye_b = (ii == jj).astype(BF16)
    gn = gn_ref[...]

    row0 = pl.multiple_of(t * npc, npc)
    g_rows = jnp.concatenate([g_ref[0, hh, pl.ds(row0, npc), :] for hh in range(hb)], axis=0)
    b_rows = jnp.concatenate([b_ref[0, hh, pl.ds(row0, npc), :] for hh in range(hb)], axis=0)
    g3 = _split3(g_rows)
    b3 = _split3(b_rows)
    dcum_rows = _dot(g3[0], triu_b) + (_dot(g3[1], triu_b) + _dot(g3[2], triu_b))
    dcum_cols = _dot(tril_b, g3[0], _NT) + (_dot(tril_b, g3[1], _NT) + _dot(tril_b, g3[2], _NT))
    b_cols = _dot(eye_b, b3[0], _NT) + (_dot(eye_b, b3[1], _NT) + _dot(eye_b, b3[2], _NT))

    ks, qs, decs, rhss, dcs, bcs = [], [], [], [], [], []
    for j in range(npc):
        rs = slice(j * c, (j + 1) * c)
        for hh in range(hb):
            hs = slice(hh * HEAD, (hh + 1) * HEAD)
            gi = hh * npc + j
            k = k_ref[0, rs, hs]
            d_col = dcum_cols[:, gi:gi + 1]
            b_col = b_cols[:, gi:gi + 1]
            diff = d_col - dcum_rows[gi:gi + 1, :]
            decs.append(jnp.where(tril, jnp.exp(jnp.where(tril, diff, 0.0)), 0.0))
            rhss.append(jnp.concatenate([v_ref[0, rs, hs] * b_col,
                                         k * (b_col * jnp.exp(d_col))], axis=1))
            ks.append(k)
            qs.append(q_ref[0, rs, hs])
            dcs.append(d_col)
            bcs.append(b_col)
    kb = jnp.stack(ks)
    qb = jnp.stack(qs)
    dec = jnp.stack(decs)
    d_colb = jnp.stack(dcs)
    d_lastb = d_colb[:, c - 1:c, :]
    a_mat = jnp.stack(bcs) * _bdot(kb, kb, _BMM_NT, passes=INV_PASSES) * jnp.where(stril, dec, 0.0)
    sol = _bdot(_unit_lower_inverse(a_mat, ii, jj), jnp.stack(rhss), passes=INV_PASSES)
    attn = _bdot(qb, kb, _BMM_NT) * dec
    wq = jnp.concatenate([sol[:, :, HEAD:], qb * jnp.exp(d_colb)], axis=1)
    k_dec = kb * jnp.exp(d_lastb - d_colb)
    g_last = jnp.exp(d_lastb)

    s_mat = s_ref[...]
    for j in range(npc):
        ps = slice(j * hb, (j + 1) * hb)
        rs = slice(j * c, (j + 1) * c)
        wq_s = _bdot(wq[ps], s_mat)
        v_new = sol[ps, :, :HEAD] - wq_s[:, :c]
        o = wq_s[:, c:] + _bdot(attn[ps], v_new)
        s_mat = s_mat * g_last[ps] + _bdot(k_dec[ps], v_new, _BMM_TN)
        ms = jnp.mean(o * o, axis=-1, keepdims=True)
        on = o * lax.rsqrt(ms + EPS) * gn
        for hh in range(hb):
            hs = slice(hh * HEAD, (hh + 1) * HEAD)
            z = z_ref[0, rs, hs]
            o_ref[0, rs, hs] = (on[hh] * (z * jax.nn.sigmoid(z))).astype(o_ref.dtype)
    s_ref[...] = s_mat

    @pl.when(t == nt - 1)
    def _():
        sf_ref[0] = s_ref[...]


def delta_prompt(qkv, proj, g, beta, s0, gnorm, n_heads, cd, hb=8, tl=512):
    b, l, _ = qkv.shape
    tl, hb = min(tl, l), min(hb, n_heads)
    nhg = n_heads // hb
    wb = hb * HEAD
    nc = l // CHUNK
    z_off = cd // wb
    return pl.pallas_call(
        functools.partial(_delta_body, hb=hb, tl=tl),
        grid=(b, nhg, l // tl),
        in_specs=[pl.BlockSpec((1, tl, wb), lambda i, h, t: (i, t, h)),
                  pl.BlockSpec((1, tl, wb), lambda i, h, t: (i, t, nhg + h)),
                  pl.BlockSpec((1, tl, wb), lambda i, h, t: (i, t, 2 * nhg + h)),
                  pl.BlockSpec((1, tl, wb), lambda i, h, t: (i, t, z_off + h)),
                  pl.BlockSpec((1, hb, nc, CHUNK), lambda i, h, t: (i, h, 0, 0)),
                  pl.BlockSpec((1, hb, nc, CHUNK), lambda i, h, t: (i, h, 0, 0)),
                  pl.BlockSpec((1, hb, HEAD, HEAD), lambda i, h, t: (i, h, 0, 0)),
                  pl.BlockSpec((1, HEAD), lambda i, h, t: (0, 0))],
        out_specs=[pl.BlockSpec((1, tl, wb), lambda i, h, t: (i, t, h)),
                   pl.BlockSpec((1, hb, HEAD, HEAD), lambda i, h, t: (i, h, 0, 0))],
        out_shape=[jax.ShapeDtypeStruct((b, l, n_heads * HEAD), BF16),
                   jax.ShapeDtypeStruct((b, n_heads, HEAD, HEAD), F32)],
        scratch_shapes=[pltpu.VMEM((hb, HEAD, HEAD), F32)],
        compiler_params=_cparams(("parallel", "parallel", "arbitrary")),
        name="delta_prompt",
    )(qkv, qkv, qkv, proj, g, beta, s0, gnorm.reshape(1, HEAD))


def _delta_dec_body(q_ref, k_ref, v_ref, z_ref, g_ref, b_ref, s0_ref, gn_ref, o_ref, sf_ref, *, nh):
    ii = lax.broadcasted_iota(jnp.int32, (HEAD, HEAD), 0)
    jj = lax.broadcasted_iota(jnp.int32, (HEAD, HEAD), 1)
    eye = ii == jj
    gn = gn_ref[...]

    def col_of(row):
        return jnp.sum(jnp.where(eye, jnp.broadcast_to(row, (HEAD, HEAD)), 0.0), axis=1, keepdims=True)

    def head(h, carry):
        q = q_ref[0, pl.ds(h, 1), :]
        k = k_ref[0, pl.ds(h, 1), :]
        v = v_ref[0, pl.ds(h, 1), :]
        z = z_ref[0, pl.ds(h, 1), :]
        eg = jnp.exp(g_ref[0, pl.ds(h, 1), :])
        beta = b_ref[0, pl.ds(h, 1), :]
        s_mat = s0_ref[0, h]
        k_col = col_of(k)
        ks = jnp.sum(k_col * s_mat, axis=0, keepdims=True)
        qs = jnp.sum(col_of(q) * s_mat, axis=0, keepdims=True)
        v_new = v * beta - (beta * eg) * ks
        qk = jnp.sum(q * k, axis=-1, keepdims=True)
        o = eg * qs + qk * v_new
        sf_ref[0, h] = s_mat * eg + k_col * v_new
        ms = jnp.mean(o * o, axis=-1, keepdims=True)
        on = o * lax.rsqrt(ms + EPS) * gn
        o_ref[0, pl.ds(h, 1), :] = (on * (z * jax.nn.sigmoid(z))).astype(o_ref.dtype)
        return carry

    lax.fori_loop(0, nh, head, 0)


def delta_decode(q, k, v, z, g_rep, b_rep, s0, gnorm):
    b, nh, _ = q.shape
    vec = pl.BlockSpec((1, nh, HEAD), lambda i: (i, 0, 0))
    st = pl.BlockSpec((1, nh, HEAD, HEAD), lambda i: (i, 0, 0, 0))
    return pl.pallas_call(
        functools.partial(_delta_dec_body, nh=nh),
        grid=(b,),
        in_specs=[vec, vec, vec, vec, vec, vec, st, pl.BlockSpec((1, HEAD), lambda i: (0, 0))],
        out_specs=[vec, st],
        out_shape=[jax.ShapeDtypeStruct((b, nh, HEAD), F32),
                   jax.ShapeDtypeStruct((b, nh, HEAD, HEAD), F32)],
        compiler_params=_cparams(("parallel",)),
        name="delta_decode",
    )(q, k, v, z, g_rep, b_rep, s0, gnorm.reshape(1, HEAD))


def _cumsum_body(*refs, pps):
    x_refs = refs[1:1 + pps]
    o_ref, tot_ref, carry_ref = refs[1 + pps:]
    p = pl.program_id(1)
    npg = pl.num_programs(1)

    @pl.when(p == 0)
    def _():
        carry_ref[...] = jnp.zeros_like(carry_ref)

    n = x_refs[0].shape[1]
    tri = (lax.broadcasted_iota(jnp.int32, (n, n), 0)
           >= lax.broadcasted_iota(jnp.int32, (n, n), 1)).astype(BF16)
    carry = carry_ref[...]
    for r, x_ref in enumerate(x_refs):
        hi, mid, lo = _split3(x_ref[0])
        out = (_dot(tri, hi) + (_dot(tri, mid) + _dot(tri, lo))) + carry
        o_ref[0, r * n:(r + 1) * n, :] = out
        carry = out[n - 1:n, :]
    carry_ref[...] = carry

    @pl.when(p == npg - 1)
    def _():
        tot_ref[0] = carry


def paged_cumsum(pages, table):
    b, npg = table.shape
    _, pg, h = pages.shape
    pps = next(c for c in (8, 4, 2, 1) if npg % c == 0)
    page_spec = lambda r: pl.BlockSpec((1, pg, h), lambda i, p, pt: (pt[i, p * pps + r], 0, 0))
    return pl.pallas_call(
        functools.partial(_cumsum_body, pps=pps),
        grid_spec=pltpu.PrefetchScalarGridSpec(
            num_scalar_prefetch=1,
            grid=(b, npg // pps),
            in_specs=[page_spec(r) for r in range(pps)],
            out_specs=[pl.BlockSpec((1, pps * pg, h), lambda i, p, pt: (i, p, 0)),
                       pl.BlockSpec((1, 1, h), lambda i, p, pt: (i, 0, 0))],
            scratch_shapes=[pltpu.VMEM((1, h), F32)]),
        out_shape=[jax.ShapeDtypeStruct((b, npg * pg, h), F32),
                   jax.ShapeDtypeStruct((b, 1, h), F32)],
        compiler_params=_cparams(("parallel", "arbitrary")),
        name="paged_cumsum",
    )(table, *([pages] * pps))


def _lane_pack3(x, nh):
    pieces = _split3(x)
    row = lax.broadcasted_iota(jnp.int32, (nh, HEAD), 0)
    lane = lax.broadcasted_iota(jnp.int32, (nh, HEAD), 1)
    out = None
    for gi, pc in enumerate(pieces):
        part = _dot(pc, (lane == gi * nh + row).astype(BF16))
        out = part if out is None else out + part
    return out


def _bias_cols_body(cum_ref, qx_ref, kx_ref, *, nh):
    h = pl.program_id(1)
    packed = _lane_pack3(cum_ref[0], nh).astype(BF16)
    r = lax.broadcasted_iota(jnp.int32, (HEAD, HEAD), 0)
    j = lax.broadcasted_iota(jnp.int32, (HEAD, HEAD), 1)
    piece = lax.div(r, nh)
    mine = jnp.logical_and(lax.rem(r, nh) == h, piece < 3)
    to_k = jnp.logical_and(mine, j == piece).astype(BF16)
    to_q = jnp.logical_and(mine, j == piece + 3).astype(BF16)
    lane = lax.broadcasted_iota(jnp.int32, (1, HEAD), 1)
    kx_ref[0, 0] = (jnp.logical_and(lane >= 3, lane < 6).astype(F32) - _dot(packed, to_k)).astype(BF16)
    qx_ref[0, 0] = ((lane < 3).astype(F32) + _dot(packed, to_q)).astype(BF16)


def fox_bias_cols(cum, n_heads):
    b, l, _ = cum.shape
    out = pl.BlockSpec((1, 1, l, HEAD), lambda i, h: (i, h, 0, 0))
    return pl.pallas_call(
        functools.partial(_bias_cols_body, nh=n_heads),
        grid=(b, n_heads),
        in_specs=[pl.BlockSpec((1, l, n_heads), lambda i, h: (i, 0, 0))],
        out_specs=[out, out],
        out_shape=[jax.ShapeDtypeStruct((b, n_heads, l, HEAD), BF16)] * 2,
        compiler_params=_cparams(("parallel", "parallel")),
        name="fox_bias_cols",
    )(cum)


def _flash_body(q_ref, k_ref, v_ref, qx_ref, kx_ref, o_ref, *, tq):
    nq = q_ref.shape[1] // tq
    causal = (lax.broadcasted_iota(jnp.int32, (tq, tq), 0)
              >= lax.broadcasted_iota(jnp.int32, (tq, tq), 1))
    for qi in range(nq):
        rq = slice(qi * tq, (qi + 1) * tq)
        qa = jnp.concatenate([q_ref[0, rq, :], qx_ref[0, 0, rq, :]], axis=1)
        m = jnp.full((tq, 1), NEG, F32)
        lsum = jnp.zeros((tq, 1), F32)
        acc = jnp.zeros((tq, HEAD), F32)
        for ki in range(qi + 1):
            rk = slice(ki * tq, (ki + 1) * tq)
            ka = jnp.concatenate([k_ref[0, rk, :], kx_ref[0, 0, rk, :]], axis=1)
            s = _dot(qa, ka, _NT)
            if ki == qi:
                s = jnp.where(causal, s, NEG)
            m_new = jnp.maximum(m, jnp.max(s, axis=1, keepdims=True))
            alpha = jnp.exp(m - m_new)
            p = jnp.exp(s - m_new)
            lsum = alpha * lsum + jnp.sum(p, axis=1, keepdims=True)
            acc = alpha * acc + _dot(p.astype(BF16), v_ref[0, rk, :])
            m = m_new
        o_ref[0, rq, :] = (acc / lsum).astype(o_ref.dtype)


def fox_prompt(q, k, v, qx, kx, n_heads, tq=512):
    b, l, _ = q.shape
    tq = min(tq, l)
    tok = pl.BlockSpec((1, l, HEAD), lambda i, h: (i, 0, h))
    col = pl.BlockSpec((1, 1, l, HEAD), lambda i, h: (i, h, 0, 0))
    return pl.pallas_call(
        functools.partial(_flash_body, tq=tq),
        grid=(b, n_heads),
        in_specs=[tok, tok, tok, col, col],
        out_specs=tok,
        out_shape=jax.ShapeDtypeStruct(q.shape, BF16),
        compiler_params=_cparams(("parallel", "parallel")),
        name="fox_prompt",
    )(q, k, v, qx, kx)


def _fox_dec_body(*refs, scale, nh, pps):
    q_ref = refs[1]
    kc_refs = refs[2:2 + pps]
    vc_refs = refs[2 + pps:2 + 2 * pps]
    ck_ref, tot_ref, lf_ref, kn_ref, vn_ref, o_ref, m_ref, l_ref, acc_ref = refs[2 + 2 * pps:]
    npos = pps * PAGE
    p = pl.program_id(1)
    npg = pl.num_programs(1)

    @pl.when(p == 0)
    def _():
        m_ref[...] = jnp.full_like(m_ref, NEG)
        l_ref[...] = jnp.zeros_like(l_ref)
        acc_ref[...] = jnp.zeros_like(acc_ref)

    q = q_ref[0] * scale
    ones = jnp.ones((HEAD, HEAD), BF16)
    sub = lax.broadcasted_iota(jnp.int32, (nh, HEAD), 0)
    lane = lax.broadcasted_iota(jnp.int32, (nh, HEAD), 1)
    own = jnp.logical_and(lax.rem(lane, nh) == sub, lane < 3 * nh)

    def head_rows(packed):
        r = packed.shape[0]
        z = jnp.where(own[None], jnp.broadcast_to(packed[:, None, :], (r, nh, HEAD)), 0.0)
        return _dot(z.reshape(r * nh, HEAD).astype(BF16), ones).reshape(r, nh, HEAD)

    cq = head_rows(_lane_pack3(jnp.broadcast_to(tot_ref[0] + lf_ref[0], (8, nh)), nh))[0]
    ck = head_rows(_lane_pack3(ck_ref[0], nh))
    kc = jnp.concatenate([r[...].reshape(PAGE, nh, HEAD) for r in kc_refs], axis=0)
    vc = jnp.concatenate([r[...].reshape(PAGE, nh, HEAD) for r in vc_refs], axis=0)
    qk = _dot((kc * q[None]).reshape(npos * nh, HEAD).astype(BF16), ones).reshape(npos, nh, HEAD)
    s = qk + (cq[None] - ck)
    m_prev = m_ref[...]
    m_new = jnp.maximum(m_prev, jnp.max(s, axis=0))
    alpha = jnp.exp(m_prev - m_new)
    pe = jnp.exp(s - m_new[None])
    l_new = alpha * l_ref[...] + jnp.sum(pe, axis=0)
    acc_new = alpha * acc_ref[...] + jnp.sum(pe * vc, axis=0)
    m_ref[...] = m_new
    l_ref[...] = l_new
    acc_ref[...] = acc_new

    @pl.when(p == npg - 1)
    def _():
        s_n = jnp.sum(kn_ref[0] * q, axis=-1, keepdims=True) + (cq - cq)
        m_fin = jnp.maximum(m_new, s_n)
        a_fin = jnp.exp(m_new - m_fin)
        p_n = jnp.exp(s_n - m_fin)
        l_fin = a_fin * l_new + p_n
        o_ref[0] = (a_fin * acc_new + p_n * vn_ref[0]) / l_fin


def fox_decode(q, k_cache, v_cache, table, ck_past, total, logf_new, k_new, v_new, n_heads):
    b, npg = table.shape
    rows = PAGE * n_heads
    pps = 2 if npg % 2 == 0 else 1
    vec = pl.BlockSpec((1, n_heads, HEAD), lambda i, p, pt: (i, 0, 0))
    row = pl.BlockSpec((1, 1, n_heads), lambda i, p, pt: (i, 0, 0))
    pages = [pl.BlockSpec((rows, HEAD), lambda i, p, pt, r=r: (pt[i, p * pps + r], 0))
             for r in range(pps)]
    return pl.pallas_call(
        functools.partial(_fox_dec_body, scale=HEAD ** -0.5, nh=n_heads, pps=pps),
        grid_spec=pltpu.PrefetchScalarGridSpec(
            num_scalar_prefetch=1,
            grid=(b, npg // pps),
            in_specs=[vec] + pages + pages
            + [pl.BlockSpec((1, pps * PAGE, n_heads), lambda i, p, pt: (i, p, 0)),
               row, row, vec, vec],
            out_specs=vec,
            scratch_shapes=[pltpu.VMEM((n_heads, HEAD), F32), pltpu.VMEM((n_heads, HEAD), F32),
                            pltpu.VMEM((n_heads, HEAD), F32)]),
        out_shape=jax.ShapeDtypeStruct((b, n_heads, HEAD), F32),
        compiler_params=_cparams(("parallel", "arbitrary")),
        name="fox_decode",
    )(table, q, *([k_cache] * pps), *([v_cache] * pps), ck_past, total, logf_new, k_new, v_new)


class _Weights:
    def __init__(self, big, a_w_in, w_f, n_main):
        self.f32 = big
        self.bf16 = {}
        self.n_main = n_main
        n_ab = a_w_in.shape[-1] - n_main
        self.w_ab = jnp.pad(a_w_in[:, :, n_main:], ((0, 0), (0, 0), (0, HEAD - n_ab))).astype(BF16)
        self.w_f = jnp.pad(w_f, ((0, 0), (0, HEAD - w_f.shape[1]))).astype(BF16)

    def mm(self, name, li, a, **kw):
        n = self.n_main if name == "w_in" else None
        if (name, li) in self.bf16:
            return matmul(a, self.bf16[name, li], **kw)
        out = matmul(a, self.f32[name], layer=li, n=n, emit_w=True, **kw)
        self.bf16[name, li] = out[-1]
        return out[0] if len(out) == 2 else out[:-1]


def _trunk(x, conv_state, delta_state, past, wts, prm):
    b, l, d = x.shape
    m = b * l
    n_a = prm["a_log"].shape[0]
    n_h = prm["a_log"].shape[1]
    qk_dim = n_h * HEAD
    conv_dim = prm["a_w_conv"].shape[-1]
    decode = past is not None
    h = x.reshape(m, d)
    pend = None

    def norm(g):
        nonlocal h, pend
        if pend is None:
            return rmsnorm(h, g)
        xn, h = rmsnorm(h, g, add=pend, emit_sum=True)
        pend = None
        return xn

    def mlp(li):
        hid = wts.mm("w_up", li, norm(prm["norm_mlp_g"][li]), out_dtypes=(BF16,), relu2=True)
        return wts.mm("w_down", li, hid)

    new_conv, new_delta = [], []
    for li in range(n_a):
        xn = norm(prm["norm_mix_g"][li])
        ab = matmul(xn, wts.w_ab, layer=li)
        gb = gdn_gates(ab, prm["a_log"][li], prm["a_dt_bias"][li])
        g, beta = gb[:, :n_h], gb[:, n_h:2 * n_h]
        if decode:
            proj = wts.mm("w_in", li, xn)
            buf = conv_state[li]
            qkv = conv_decode(proj, buf, prm["a_w_conv"][li], qk_dim)
            new_conv.append(jnp.concatenate([buf[:, 1:], proj[:, None, :conv_dim]], axis=1))
            hv = lambda t: t.reshape(b, n_h, HEAD)
            rep = lambda t: jnp.broadcast_to(t[:, :, None], (b, n_h, HEAD))
            o, s_new = delta_decode(hv(qkv[:, :qk_dim]), hv(qkv[:, qk_dim:2 * qk_dim]),
                                    hv(qkv[:, 2 * qk_dim:]), hv(proj[:, conv_dim:]),
                                    rep(g), rep(beta), delta_state[li], prm["a_o_norm_g"][li])
            o = o.reshape(m, n_h * HEAD).astype(BF16)
        else:
            buf8 = jnp.pad(conv_state[li], ((0, 0), (8 - (CONV_W - 1), 0), (0, 0)))
            proj, tail = in_proj_conv(xn, wts.bf16["w_in", li], buf8, prm["a_w_conv"][li],
                                      qk_dim, l, wts.n_main)
            proj3 = proj.reshape(b, l, -1)
            new_conv.append(tail[:, 8 - (CONV_W - 1):, :])
            chunks = lambda t: t.reshape(b, l // CHUNK, CHUNK, n_h).transpose(0, 3, 1, 2)
            o, s_new = delta_prompt(proj3, proj3, chunks(g), chunks(beta), delta_state[li],
                                    prm["a_o_norm_g"][li], n_h, conv_dim)
            o = o.reshape(m, n_h * HEAD)
        new_delta.append(s_new)
        h = wts.mm("w_out", li, o, res=h)
        pend = mlp(li)

    hn = norm(prm["kv_norm_g"])
    nb = wts.f32["w_k"].shape[1]
    n_hb = nb // HEAD
    k_new, k_bf = wts.mm("w_k", None, hn, out_dtypes=(F32, BF16))
    v_new, v_bf = wts.mm("w_v", None, hn, out_dtypes=(F32, BF16))
    logf = bias_log_sigmoid(matmul(hn, wts.w_f), prm["b_f"])[:, :n_hb]
    if decode:
        k_cache, v_cache, logf_cache, table = past
        ck_past, total = paged_cumsum(logf_cache, table)
        k_cache = k_cache.reshape(-1, HEAD)
        v_cache = v_cache.reshape(-1, HEAD)
    else:
        npg = l // PAGE
        table = jnp.arange(b * npg, dtype=jnp.int32).reshape(b, npg)
        cum, _ = paged_cumsum(logf.reshape(b * npg, PAGE, n_hb), table)
        qx, kx = fox_bias_cols(cum, n_hb)
        k_bf = k_bf.reshape(b, l, nb)
        v_bf = v_bf.reshape(b, l, nb)
    for j in range(prm["b_w_q"].shape[0]):
        li = n_a + j
        xn = norm(prm["norm_mix_g"][li])
        if decode:
            q = wts.mm("w_q", j, xn)
            hv = lambda t: t.reshape(b, n_hb, HEAD)
            o = fox_decode(hv(q), k_cache, v_cache, table, ck_past, total,
                           logf.reshape(b, 1, n_hb), hv(k_new), hv(v_new), n_hb).astype(BF16)
        else:
            q = wts.mm("w_q", j, xn, out_dtypes=(BF16,), scale=HEAD ** -0.5)
            o = fox_prompt(q.reshape(b, l, nb), k_bf, v_bf, qx, kx, n_hb)
        h = wts.mm("w_o", j, o.reshape(m, nb), res=h)
        pend = mlp(li)
    y = rmsnorm(h, prm["final_norm_g"], out_dtype=F32, add=pend)
    return (y.reshape(b, l, d), jnp.stack(new_delta), jnp.stack(new_conv),
            k_new.reshape(b, l, n_hb, HEAD), v_new.reshape(b, l, n_hb, HEAD),
            logf.reshape(b, l, n_hb))


def kernel(x_prompt, x_sample, cache_k, cache_v, cache_logf, state_delta, state_conv, page_table,
           norm_mix_g, norm_mlp_g, w_up, w_down, a_w_in, a_w_conv, a_log, a_dt_bias, a_o_norm_g,
           a_w_out, kv_norm_g, w_k, w_v, w_f, b_f, b_w_q, b_w_o, final_norm_g):
    conv_dim = a_w_conv.shape[-1]
    v_dim = a_w_out.shape[1]
    wts = _Weights(dict(w_up=w_up, w_down=w_down, w_in=a_w_in, w_out=a_w_out, w_k=w_k, w_v=w_v,
                        w_q=b_w_q, w_o=b_w_o), a_w_in, w_f, conv_dim + v_dim)
    prm = dict(norm_mix_g=norm_mix_g, norm_mlp_g=norm_mlp_g, a_w_conv=a_w_conv, a_log=a_log,
               a_dt_bias=a_dt_bias, a_o_norm_g=a_o_norm_g, kv_norm_g=kv_norm_g, b_f=b_f,
               b_w_q=b_w_q, final_norm_g=final_norm_g)
    n_a = a_log.shape[0]
    n_p = x_prompt.shape[0]
    n_h = a_log.shape[1]
    zero_conv = jnp.zeros((n_a, n_p, CONV_W - 1, conv_dim), state_conv.dtype)
    zero_delta = jnp.zeros((n_a, n_p, n_h, HEAD, HEAD), state_delta.dtype)
    y_s, delta_s, conv_s, k_s, v_s, logf_s = _trunk(
        x_sample, state_conv, state_delta, (cache_k, cache_v, cache_logf, page_table), wts, prm)
    y_p, delta_p, conv_p, k_p, v_p, logf_p = _trunk(x_prompt, zero_conv, zero_delta, None, wts, prm)
    return (y_p, y_s, delta_p, conv_p, k_p, v_p, logf_p, delta_s, conv_s, k_s, v_s, logf_s)
```

```python
import functools

import jax
import jax.numpy as jnp
from jax import lax
from jax.experimental import pallas as pl
from jax.experimental.pallas import tpu as pltpu

F32 = jnp.float32
BF16 = jnp.bfloat16

EPS = 1e-6
HEAD = 128
CHUNK = 64
CONV_W = 4
PAGE = 128
NEG = -1e30
V7X_VMEM_LIMIT = 56 * 1024 * 1024


def _cparams(sem):
    return pltpu.CompilerParams(dimension_semantics=sem, vmem_limit_bytes=V7X_VMEM_LIMIT)


def _rms_body(*refs, has_add, emit_sum):
    x = refs[0][...]
    if has_add:
        x = x + refs[1][...]
    g_ref = refs[1 + has_add]
    outs = refs[2 + has_add:]
    ms = jnp.mean(x * x, axis=-1, keepdims=True)
    outs[0][...] = (x * lax.rsqrt(ms + EPS) * g_ref[...]).astype(outs[0].dtype)
    if emit_sum:
        outs[1][...] = x


def rmsnorm(x, g, out_dtype=BF16, add=None, emit_sum=False):
    m, d = x.shape
    tm = min(m, 256 if add is not None else 512)
    row = pl.BlockSpec((tm, d), lambda i: (i, 0))
    ins = [x] + ([add] if add is not None else [])
    out = pl.pallas_call(
        functools.partial(_rms_body, has_add=add is not None, emit_sum=emit_sum),
        grid=(m // tm,),
        in_specs=[row] * len(ins) + [pl.BlockSpec((1, d), lambda i: (0, 0))],
        out_specs=[row] * (1 + emit_sum),
        out_shape=[jax.ShapeDtypeStruct((m, d), out_dtype)]
        + ([jax.ShapeDtypeStruct((m, d), F32)] if emit_sum else []),
        compiler_params=_cparams(("parallel",)),
        name="rmsnorm",
    )(*ins, g.reshape(1, d))
    return tuple(out) if emit_sum else out[0]


def _mm_body(*refs, nk, relu2, scale, has_res, n_out, emit_w):
    a_ref, w_ref = refs[0], refs[1]
    pos = 2
    r_ref = None
    if has_res:
        r_ref = refs[pos]
        pos += 1
    o_refs = refs[pos:pos + n_out]
    pos += n_out
    w = w_ref[...]
    if emit_w:
        w = w.astype(BF16)
        refs[pos][...] = w
        pos += 1
    scr = refs[pos:]

    def finish(acc):
        if relu2:
            acc = jnp.maximum(acc, 0.0)
            acc = acc * acc
        if scale is not None:
            acc = acc * scale
        if has_res:
            acc = r_ref[...] + acc
        for o_ref in o_refs:
            o_ref[...] = acc.astype(o_ref.dtype)

    part = jnp.dot(a_ref[...], w, preferred_element_type=F32)
    if nk == 1:
        finish(part)
    else:
        acc_ref = scr[0]
        k = pl.program_id(2)

        @pl.when(k == 0)
        def _():
            acc_ref[...] = part

        @pl.when(jnp.logical_and(k > 0, k < nk - 1))
        def _():
            acc_ref[...] += part

        @pl.when(k == nk - 1)
        def _():
            finish(acc_ref[...] + part)


def matmul(a, w, out_dtypes=(F32,), res=None, relu2=False, scale=None, layer=None, n=None,
           emit_w=False, tm=1024, tn=1024, tk=4096):
    m, kd = a.shape
    n = w.shape[-1] if n is None else n
    if kd > tk and (res is not None or emit_w):
        tk //= 2
    tm, tn, tk = min(tm, m), min(tn, n), min(tk, kd)
    nk = kd // tk
    assert not emit_w or m == tm
    if layer is None:
        w_spec = pl.BlockSpec((tk, tn), lambda i, j, k: (k, j))
    else:
        w_spec = pl.BlockSpec((None, tk, tn), lambda i, j, k: (layer, k, j))
    in_specs = [pl.BlockSpec((tm, tk), lambda i, j, k: (i, k)), w_spec]
    args = [a, w]
    if res is not None:
        in_specs.append(pl.BlockSpec((tm, tn), lambda i, j, k: (i, j)))
        args.append(res)
    out_specs = [pl.BlockSpec((tm, tn), lambda i, j, k: (i, j)) for _ in out_dtypes]
    out_shape = [jax.ShapeDtypeStruct((m, n), dt) for dt in out_dtypes]
    if emit_w:
        out_specs.append(pl.BlockSpec((tk, tn), lambda i, j, k: (k, j)))
        out_shape.append(jax.ShapeDtypeStruct((kd, n), BF16))
    out = pl.pallas_call(
        functools.partial(_mm_body, nk=nk, relu2=relu2, scale=scale, has_res=res is not None,
                          n_out=len(out_dtypes), emit_w=emit_w),
        grid=(m // tm, n // tn, nk),
        in_specs=in_specs,
        out_specs=out_specs,
        out_shape=out_shape,
        scratch_shapes=[pltpu.VMEM((tm, tn), F32)] if nk > 1 else [],
        compiler_params=_cparams(("parallel", "parallel", "arbitrary")),
        name="matmul",
    )(*args)
    return out[0] if len(out) == 1 else tuple(out)


def _softplus(x):
    return jnp.maximum(x, 0.0) + jnp.log1p(jnp.exp(-jnp.abs(x)))


def _gates_body(x_ref, alog_ref, dt_ref, o_ref, *, nh):
    x = x_ref[...]
    lane = lax.broadcasted_iota(jnp.int32, x.shape, 1)
    g = -jnp.exp(alog_ref[...]) * _softplus(x + dt_ref[...])
    beta = jax.nn.sigmoid(x)
    o_ref[...] = jnp.where(lane < nh, g, beta)


def gdn_gates(ab, a_log, dt_bias):
    m, w = ab.shape
    nh = a_log.shape[0]
    tm = min(m, 1024)
    pad = lambda v: jnp.pad(v.astype(F32), (0, w - nh)).reshape(1, w)
    return pl.pallas_call(
        functools.partial(_gates_body, nh=nh),
        grid=(m // tm,),
        in_specs=[pl.BlockSpec((tm, w), lambda i: (i, 0)),
                  pl.BlockSpec((1, w), lambda i: (0, 0)),
                  pl.BlockSpec((1, w), lambda i: (0, 0))],
        out_specs=pl.BlockSpec((tm, w), lambda i: (i, 0)),
        out_shape=jax.ShapeDtypeStruct((m, w), F32),
        compiler_params=_cparams(("parallel",)),
        name="gdn_gates",
    )(ab, pad(a_log), pad(dt_bias))


def _logsig_body(x_ref, b_ref, o_ref):
    y = x_ref[...] + b_ref[...]
    o_ref[...] = -_softplus(-y)


def bias_log_sigmoid(x, b):
    m, w = x.shape
    tm = min(m, 1024)
    return pl.pallas_call(
        _logsig_body,
        grid=(m // tm,),
        in_specs=[pl.BlockSpec((tm, w), lambda i: (i, 0)),
                  pl.BlockSpec((1, w), lambda i: (0, 0))],
        out_specs=pl.BlockSpec((tm, w), lambda i: (i, 0)),
        out_shape=jax.ShapeDtypeStruct((m, w), F32),
        compiler_params=_cparams(("parallel",)),
        name="bias_log_sigmoid",
    )(x, jnp.pad(b.astype(F32), (0, w - b.shape[0])).reshape(1, w))


def _l2_heads(y, o_ref, idx, qscale):
    for j in range(y.shape[-1] // HEAD):
        ys = y[..., j * HEAD:(j + 1) * HEAD]
        ss = jnp.sum(ys * ys, axis=-1, keepdims=True)
        o_ref[idx + (slice(j * HEAD, (j + 1) * HEAD),)] = ys * lax.rsqrt(ss + EPS) * qscale


def _in_conv_body(a_ref, w_ref, wc_ref, buf_ref, o_ref, tail_ref, ext_ref, halo_ref,
                  *, tm, n_q, n_conv, tiles_per_seq):
    i = pl.program_id(0)
    j = pl.program_id(1)
    jc = jnp.minimum(j, n_conv)
    first = lax.rem(i, tiles_per_seq) == 0
    nsub, _, sub = ext_ref.shape[0], ext_ref.shape[1], ext_ref.shape[2]

    @pl.when(first)
    def _():
        for s in range(nsub):
            ext_ref[s, 0:8, :] = buf_ref[0, :, s * sub:(s + 1) * sub]

    @pl.when(jnp.logical_not(first))
    def _():
        for s in range(nsub):
            ext_ref[s, 0:8, :] = halo_ref[jc, :, s * sub:(s + 1) * sub]

    is_conv = j < n_conv
    is_qk = j < 2 * n_q
    qscale = jnp.where(j < n_q, HEAD ** -0.5, 1.0).astype(F32)

    def finish(s, acc):
        cs = slice(s * sub, (s + 1) * sub)
        ext_ref[s, 8:8 + tm, :] = acc
        wc = wc_ref[:, cs]
        cv = ext_ref[s, 5:5 + tm, :] * wc[0:1]
        cv = cv + ext_ref[s, 6:6 + tm, :] * wc[1:2]
        cv = cv + ext_ref[s, 7:7 + tm, :] * wc[2:3]
        cv = cv + acc * wc[3:4]
        y = cv * jax.nn.sigmoid(cv)
        tail = acc[tm - 8:tm]
        halo_ref[jc, :, cs] = tail
        tail_ref[0, :, cs] = tail
        for hh in range(sub // HEAD):
            hs = slice(hh * HEAD, (hh + 1) * HEAD)
            ys = y[:, hs]
            ss = jnp.sum(ys * ys, axis=-1, keepdims=True)
            inv = jnp.where(is_qk, lax.rsqrt(ss + EPS), 1.0)
            o_ref[:, s * sub + hh * HEAD:s * sub + (hh + 1) * HEAD] = jnp.where(
                is_conv, ys * inv * qscale, acc[:, hs])

    pending = None
    for s in range(nsub):
        acc = jnp.dot(a_ref[...], w_ref[:, s * sub:(s + 1) * sub], preferred_element_type=F32)
        if pending is not None:
            finish(*pending)
        pending = (s, acc)
    finish(*pending)


def in_proj_conv(xn, w, buf8, w_conv, qk_dim, seq_len, n_main, tm=1024, tn=1024):
    m, kd = xn.shape
    cd = w_conv.shape[1]
    tm, tn = min(tm, seq_len), min(tn, qk_dim)
    sub = min(tn, 256)
    tps = seq_len // tm
    n_conv = cd // tn
    cj = lambda j: jnp.minimum(j, n_conv - 1)
    out, tail = pl.pallas_call(
        functools.partial(_in_conv_body, tm=tm, n_q=qk_dim // tn, n_conv=n_conv,
                          tiles_per_seq=tps),
        grid=(m // tm, n_main // tn),
        in_specs=[pl.BlockSpec((tm, kd), lambda i, j: (i, 0)),
                  pl.BlockSpec((kd, tn), lambda i, j: (0, j)),
                  pl.BlockSpec((CONV_W, tn), lambda i, j: (0, cj(j))),
                  pl.BlockSpec((1, 8, tn), lambda i, j: (i // tps, 0, cj(j)))],
        out_specs=[pl.BlockSpec((tm, tn), lambda i, j: (i, j)),
                   pl.BlockSpec((1, 8, tn), lambda i, j: (i, 0, jnp.minimum(j, n_conv)))],
        out_shape=[jax.ShapeDtypeStruct((m, n_main), F32),
                   jax.ShapeDtypeStruct((m // tm, 8, cd + tn), F32)],
        scratch_shapes=[pltpu.VMEM((tn // sub, tm + 8, sub), F32),
                        pltpu.VMEM((n_conv + 1, 8, tn), F32)],
        compiler_params=_cparams(("arbitrary", "arbitrary")),
        name="in_proj_conv",
    )(xn, w, w_conv, buf8)
    return out, tail.reshape(m // seq_len, tps, 8, cd + tn)[:, tps - 1, :, :cd]


def _conv_dec_body(u_ref, buf_ref, w_ref, o_ref, *, n_q_tiles):
    c = pl.program_id(0)
    w = w_ref[...]
    acc = buf_ref[:, 0, :] * w[0:1]
    acc = acc + buf_ref[:, 1, :] * w[1:2]
    acc = acc + buf_ref[:, 2, :] * w[2:3]
    acc = acc + u_ref[...] * w[3:4]
    y = acc * jax.nn.sigmoid(acc)

    @pl.when(c < 2 * n_q_tiles)
    def _():
        qscale = jnp.where(c < n_q_tiles, HEAD ** -0.5, 1.0).astype(F32)
        _l2_heads(y, o_ref, (slice(None),), qscale)

    @pl.when(c >= 2 * n_q_tiles)
    def _():
        o_ref[...] = y


def conv_decode(proj, buf, w_conv, qk_dim, tc=512):
    b = proj.shape[0]
    cd = w_conv.shape[1]
    tc = min(tc, qk_dim)
    return pl.pallas_call(
        functools.partial(_conv_dec_body, n_q_tiles=qk_dim // tc),
        grid=(cd // tc,),
        in_specs=[pl.BlockSpec((b, tc), lambda c: (0, c)),
                  pl.BlockSpec((b, CONV_W - 1, tc), lambda c: (0, 0, c)),
                  pl.BlockSpec((CONV_W, tc), lambda c: (0, c))],
        out_specs=pl.BlockSpec((b, tc), lambda c: (0, c)),
        out_shape=jax.ShapeDtypeStruct((b, cd), F32),
        compiler_params=_cparams(("parallel",)),
        name="conv_decode",
    )(proj, buf, w_conv)


def _split(x):
    hi = x.astype(BF16)
    lo = (x - hi.astype(F32)).astype(BF16)
    return hi, lo


def _dot(a, b, dims=(((1,), (0,)), ((), ()))):
    return lax.dot_general(a, b, dims, preferred_element_type=F32)


_NT = (((1,), (1,)), ((), ()))
_TN = (((0,), (0,)), ((), ()))


def _dot1(a, b, dims=(((1,), (0,)), ((), ()))):
    return _dot(a.astype(BF16), b.astype(BF16), dims)


def _dot3(a, b, dims=(((1,), (0,)), ((), ()))):
    ah, al = _split(a)
    bh, bl = _split(b)
    return _dot(ah, bh, dims) + (_dot(ah, bl, dims) + _dot(al, bh, dims))


_BMM = (((2,), (1,)), ((0,), (0,)))
_BMM_NT = (((2,), (2,)), ((0,), (0,)))
_BMM_TN = (((1,), (1,)), ((0,), (0,)))
INV_PASSES = 1


def _bdot(a, b, dims=_BMM, passes=1):
    if passes == 1:
        return _dot(a.astype(BF16), b.astype(BF16), dims)
    ah, al = _split(a)
    bh, bl = _split(b)
    return _dot(ah, bh, dims) + (_dot(ah, bl, dims) + _dot(al, bh, dims))


def _split3(x):
    hi = x.astype(BF16)
    r1 = x - hi.astype(F32)
    mid = r1.astype(BF16)
    lo = (r1 - mid.astype(F32)).astype(BF16)
    return hi, mid, lo


def _unit_lower_inverse(a_strict, ii, jj):
    c = a_strict.shape[-1]
    eye = (ii == jj).astype(F32)
    m = None
    s = 1
    k = 0
    while s < c:
        off = jnp.logical_and((ii >> (k + 1)) == (jj >> (k + 1)),
                              jnp.logical_and(((ii >> k) & 1) == 1, ((jj >> k) & 1) == 0))
        a_off = jnp.where(off, a_strict, 0.0)
        if m is None:
            m = eye - a_off
        else:
            m = m - _bdot(m, _bdot(a_off, m, passes=INV_PASSES), passes=INV_PASSES)
        s *= 2
        k += 1
    return m


def _delta_body(q_ref, k_ref, v_ref, z_ref, g_ref, b_ref, s0_ref, gn_ref, o_ref, sf_ref, s_ref,
                *, hb, tl):
    t = pl.program_id(2)
    nt = pl.num_programs(2)
    c = CHUNK
    npc = tl // c

    @pl.when(t == 0)
    def _():
        s_ref[...] = s0_ref[0]

    ii = lax.broadcasted_iota(jnp.int32, (c, c), 0)
    jj = lax.broadcasted_iota(jnp.int32, (c, c), 1)
    tril = ii >= jj
    stril = ii > jj
    tril_b = tril.astype(BF16)
    triu_b = (ii <= jj).astype(BF16)
    eye_b = (ii == jj).astype(BF16)
    gn = gn_ref[...]

    row0 = pl.multiple_of(t * npc, npc)
    g_rows = jnp.concatenate([g_ref[0, hh, pl.ds(row0, npc), :] for hh in range(hb)], axis=0)
    b_rows = jnp.concatenate([b_ref[0, hh, pl.ds(row0, npc), :] for hh in range(hb)], axis=0)
    g3 = _split3(g_rows)
    b3 = _split3(b_rows)
    dcum_rows = _dot(g3[0], triu_b) + (_dot(g3[1], triu_b) + _dot(g3[2], triu_b))
    dcum_cols = _dot(tril_b, g3[0], _NT) + (_dot(tril_b, g3[1], _NT) + _dot(tril_b, g3[2], _NT))
    b_cols = _dot(eye_b, b3[0], _NT) + (_dot(eye_b, b3[1], _NT) + _dot(eye_b, b3[2], _NT))

    ks, qs, decs, rhss, dcs, bcs = [], [], [], [], [], []
    for j in range(npc):
        rs = slice(j * c, (j + 1) * c)
        for hh in range(hb):
            hs = slice(hh * HEAD, (hh + 1) * HEAD)
            gi = hh * npc + j
            k = k_ref[0, rs, hs]
            d_col = dcum_cols[:, gi:gi + 1]
            b_col = b_cols[:, gi:gi + 1]
            diff = d_col - dcum_rows[gi:gi + 1, :]
            decs.append(jnp.where(tril, jnp.exp(jnp.where(tril, diff, 0.0)), 0.0))
            rhss.append(jnp.concatenate([v_ref[0, rs, hs] * b_col,
                                         k * (b_col * jnp.exp(d_col))], axis=1))
            ks.append(k)
            qs.append(q_ref[0, rs, hs])
            dcs.append(d_col)
            bcs.append(b_col)
    kb = jnp.stack(ks)
    qb = jnp.stack(qs)
    dec = jnp.stack(decs)
    d_colb = jnp.stack(dcs)
    d_lastb = d_colb[:, c - 1:c, :]
    a_mat = jnp.stack(bcs) * _bdot(kb, kb, _BMM_NT, passes=INV_PASSES) * jnp.where(stril, dec, 0.0)
    sol = _bdot(_unit_lower_inverse(a_mat, ii, jj), jnp.stack(rhss), passes=INV_PASSES)
    attn = _bdot(qb, kb, _BMM_NT) * dec
    wq = jnp.concatenate([sol[:, :, HEAD:], qb * jnp.exp(d_colb)], axis=1)
    k_dec = kb * jnp.exp(d_lastb - d_colb)
    g_last = jnp.exp(d_lastb)

    s_mat = s_ref[...]
    for j in range(npc):
        ps = slice(j * hb, (j + 1) * hb)
        rs = slice(j * c, (j + 1) * c)
        wq_s = _bdot(wq[ps], s_mat)
        v_new = sol[ps, :, :HEAD] - wq_s[:, :c]
        o = wq_s[:, c:] + _bdot(attn[ps], v_new)
        s_mat = s_mat * g_last[ps] + _bdot(k_dec[ps], v_new, _BMM_TN)
        ms = jnp.mean(o * o, axis=-1, keepdims=True)
        on = o * lax.rsqrt(ms + EPS) * gn
        for hh in range(hb):
            hs = slice(hh * HEAD, (hh + 1) * HEAD)
            z = z_ref[0, rs, hs]
            o_ref[0, rs, hs] = (on[hh] * (z * jax.nn.sigmoid(z))).astype(o_ref.dtype)
    s_ref[...] = s_mat

    @pl.when(t == nt - 1)
    def _():
        sf_ref[0] = s_ref[...]


def delta_prompt(qkv, proj, g, beta, s0, gnorm, n_heads, cd, hb=8, tl=512):
    b, l, _ = qkv.shape
    tl, hb = min(tl, l), min(hb, n_heads)
    nhg = n_heads // hb
    wb = hb * HEAD
    nc = l // CHUNK
    z_off = cd // wb
    return pl.pallas_call(
        functools.partial(_delta_body, hb=hb, tl=tl),
        grid=(b, nhg, l // tl),
        in_specs=[pl.BlockSpec((1, tl, wb), lambda i, h, t: (i, t, h)),
                  pl.BlockSpec((1, tl, wb), lambda i, h, t: (i, t, nhg + h)),
                  pl.BlockSpec((1, tl, wb), lambda i, h, t: (i, t, 2 * nhg + h)),
                  pl.BlockSpec((1, tl, wb), lambda i, h, t: (i, t, z_off + h)),
                  pl.BlockSpec((1, hb, nc, CHUNK), lambda i, h, t: (i, h, 0, 0)),
                  pl.BlockSpec((1, hb, nc, CHUNK), lambda i, h, t: (i, h, 0, 0)),
                  pl.BlockSpec((1, hb, HEAD, HEAD), lambda i, h, t: (i, h, 0, 0)),
                  pl.BlockSpec((1, HEAD), lambda i, h, t: (0, 0))],
        out_specs=[pl.BlockSpec((1, tl, wb), lambda i, h, t: (i, t, h)),
                   pl.BlockSpec((1, hb, HEAD, HEAD), lambda i, h, t: (i, h, 0, 0))],
        out_shape=[jax.ShapeDtypeStruct((b, l, n_heads * HEAD), BF16),
                   jax.ShapeDtypeStruct((b, n_heads, HEAD, HEAD), F32)],
        scratch_shapes=[pltpu.VMEM((hb, HEAD, HEAD), F32)],
        compiler_params=_cparams(("parallel", "parallel", "arbitrary")),
        name="delta_prompt",
    )(qkv, qkv, qkv, proj, g, beta, s0, gnorm.reshape(1, HEAD))


def _delta_dec_body(q_ref, k_ref, v_ref, z_ref, g_ref, b_ref, s0_ref, gn_ref, o_ref, sf_ref, *, nh):
    ii = lax.broadcasted_iota(jnp.int32, (HEAD, HEAD), 0)
    jj = lax.broadcasted_iota(jnp.int32, (HEAD, HEAD), 1)
    eye = ii == jj
    gn = gn_ref[...]

    def col_of(row):
        return jnp.sum(jnp.where(eye, jnp.broadcast_to(row, (HEAD, HEAD)), 0.0), axis=1, keepdims=True)

    def head(h, carry):
        q = q_ref[0, pl.ds(h, 1), :]
        k = k_ref[0, pl.ds(h, 1), :]
        v = v_ref[0, pl.ds(h, 1), :]
        z = z_ref[0, pl.ds(h, 1), :]
        eg = jnp.exp(g_ref[0, pl.ds(h, 1), :])
        beta = b_ref[0, pl.ds(h, 1), :]
        s_mat = s0_ref[0, h]
        k_col = col_of(k)
        ks = jnp.sum(k_col * s_mat, axis=0, keepdims=True)
        qs = jnp.sum(col_of(q) * s_mat, axis=0, keepdims=True)
        v_new = v * beta - (beta * eg) * ks
        qk = jnp.sum(q * k, axis=-1, keepdims=True)
        o = eg * qs + qk * v_new
        sf_ref[0, h] = s_mat * eg + k_col * v_new
        ms = jnp.mean(o * o, axis=-1, keepdims=True)
        on = o * lax.rsqrt(ms + EPS) * gn
        o_ref[0, pl.ds(h, 1), :] = (on * (z * jax.nn.sigmoid(z))).astype(o_ref.dtype)
        return carry

    lax.fori_loop(0, nh, head, 0)


def delta_decode(q, k, v, z, g_rep, b_rep, s0, gnorm):
    b, nh, _ = q.shape
    vec = pl.BlockSpec((1, nh, HEAD), lambda i: (i, 0, 0))
    st = pl.BlockSpec((1, nh, HEAD, HEAD), lambda i: (i, 0, 0, 0))
    return pl.pallas_call(
        functools.partial(_delta_dec_body, nh=nh),
        grid=(b,),
        in_specs=[vec, vec, vec, vec, vec, vec, st, pl.BlockSpec((1, HEAD), lambda i: (0, 0))],
        out_specs=[vec, st],
        out_shape=[jax.ShapeDtypeStruct((b, nh, HEAD), F32),
                   jax.ShapeDtypeStruct((b, nh, HEAD, HEAD), F32)],
        compiler_params=_cparams(("parallel",)),
        name="delta_decode",
    )(q, k, v, z, g_rep, b_rep, s0, gnorm.reshape(1, HEAD))


def _cumsum_body(*refs, pps):
    x_refs = refs[1:1 + pps]
    o_ref, tot_ref, carry_ref = refs[1 + pps:]
    p = pl.program_id(1)
    npg = pl.num_programs(1)

    @pl.when(p == 0)
    def _():
        carry_ref[...] = jnp.zeros_like(carry_ref)

    n = x_refs[0].shape[1]
    tri = (lax.broadcasted_iota(jnp.int32, (n, n), 0)
           >= lax.broadcasted_iota(jnp.int32, (n, n), 1)).astype(BF16)
    carry = carry_ref[...]
    for r, x_ref in enumerate(x_refs):
        hi, mid, lo = _split3(x_ref[0])
        out = (_dot(tri, hi) + (_dot(tri, mid) + _dot(tri, lo))) + carry
        o_ref[0, r * n:(r + 1) * n, :] = out
        carry = out[n - 1:n, :]
    carry_ref[...] = carry

    @pl.when(p == npg - 1)
    def _():
        tot_ref[0] = carry


def paged_cumsum(pages, table):
    b, npg = table.shape
    _, pg, h = pages.shape
    pps = next(c for c in (8, 4, 2, 1) if npg % c == 0)
    page_spec = lambda r: pl.BlockSpec((1, pg, h), lambda i, p, pt: (pt[i, p * pps + r], 0, 0))
    return pl.pallas_call(
        functools.partial(_cumsum_body, pps=pps),
        grid_spec=pltpu.PrefetchScalarGridSpec(
            num_scalar_prefetch=1,
            grid=(b, npg // pps),
            in_specs=[page_spec(r) for r in range(pps)],
            out_specs=[pl.BlockSpec((1, pps * pg, h), lambda i, p, pt: (i, p, 0)),
                       pl.BlockSpec((1, 1, h), lambda i, p, pt: (i, 0, 0))],
            scratch_shapes=[pltpu.VMEM((1, h), F32)]),
        out_shape=[jax.ShapeDtypeStruct((b, npg * pg, h), F32),
                   jax.ShapeDtypeStruct((b, 1, h), F32)],
        compiler_params=_cparams(("parallel", "arbitrary")),
        name="paged_cumsum",
    )(table, *([pages] * pps))


def _lane_pack3(x, nh):
    pieces = _split3(x)
    row = lax.broadcasted_iota(jnp.int32, (nh, HEAD), 0)
    lane = lax.broadcasted_iota(jnp.int32, (nh, HEAD), 1)
    out = None
    for gi, pc in enumerate(pieces):
        part = _dot(pc, (lane == gi * nh + row).astype(BF16))
        out = part if out is None else out + part
    return out


def _bias_cols_body(cum_ref, qx_ref, kx_ref, *, nh):
    hg = qx_ref.shape[1]
    packed = _lane_pack3(cum_ref[0], nh).astype(BF16)
    r = lax.broadcasted_iota(jnp.int32, (HEAD, HEAD), 0)
    j = lax.broadcasted_iota(jnp.int32, (HEAD, HEAD), 1)
    piece = lax.div(r, nh)
    lane = lax.broadcasted_iota(jnp.int32, (1, HEAD), 1)
    ones_k = jnp.logical_and(lane >= 3, lane < 6).astype(F32)
    ones_q = (lane < 3).astype(F32)
    for hh in range(hg):
        h = pl.program_id(1) * hg + hh
        mine = jnp.logical_and(lax.rem(r, nh) == h, piece < 3)
        to_k = jnp.logical_and(mine, j == piece).astype(BF16)
        to_q = jnp.logical_and(mine, j == piece + 3).astype(BF16)
        kx_ref[0, hh] = (ones_k - _dot(packed, to_k)).astype(BF16)
        qx_ref[0, hh] = (ones_q + _dot(packed, to_q)).astype(BF16)


def fox_bias_cols(cum, n_heads):
    b, l, _ = cum.shape
    hg = min(8, n_heads)
    out = pl.BlockSpec((1, hg, l, HEAD), lambda i, h: (i, h, 0, 0))
    return pl.pallas_call(
        functools.partial(_bias_cols_body, nh=n_heads),
        grid=(b, n_heads // hg),
        in_specs=[pl.BlockSpec((1, l, n_heads), lambda i, h: (i, 0, 0))],
        out_specs=[out, out],
        out_shape=[jax.ShapeDtypeStruct((b, n_heads, l, HEAD), BF16)] * 2,
        compiler_params=_cparams(("parallel", "parallel")),
        name="fox_bias_cols",
    )(cum)


def _flash_body(q_ref, k_ref, v_ref, qx_ref, kx_ref, o_ref, *, tq):
    nq = q_ref.shape[1] // tq
    causal = (lax.broadcasted_iota(jnp.int32, (tq, tq), 0)
              >= lax.broadcasted_iota(jnp.int32, (tq, tq), 1))
    for qi in range(nq):
        rq = slice(qi * tq, (qi + 1) * tq)
        qa = jnp.concatenate([q_ref[0, rq, :], qx_ref[0, 0, rq, :]], axis=1)
        m = jnp.full((tq, 1), NEG, F32)
        lsum = jnp.zeros((tq, 1), F32)
        acc = jnp.zeros((tq, HEAD), F32)
        for ki in range(qi + 1):
            rk = slice(ki * tq, (ki + 1) * tq)
            ka = jnp.concatenate([k_ref[0, rk, :], kx_ref[0, 0, rk, :]], axis=1)
            s = _dot(qa, ka, _NT)
            if ki == qi:
                s = jnp.where(causal, s, NEG)
            m_new = jnp.maximum(m, jnp.max(s, axis=1, keepdims=True))
            alpha = jnp.exp(m - m_new)
            p = jnp.exp(s - m_new)
            lsum = alpha * lsum + jnp.sum(p, axis=1, keepdims=True)
            acc = alpha * acc + _dot(p.astype(BF16), v_ref[0, rk, :])
            m = m_new
        o_ref[0, rq, :] = (acc / lsum).astype(o_ref.dtype)


def fox_prompt(q, k, v, qx, kx, n_heads, tq=512):
    b, l, _ = q.shape
    tq = min(tq, l)
    tok = pl.BlockSpec((1, l, HEAD), lambda i, h: (i, 0, h))
    col = pl.BlockSpec((1, 1, l, HEAD), lambda i, h: (i, h, 0, 0))
    return pl.pallas_call(
        functools.partial(_flash_body, tq=tq),
        grid=(b, n_heads),
        in_specs=[tok, tok, tok, col, col],
        out_specs=tok,
        out_shape=jax.ShapeDtypeStruct(q.shape, BF16),
        compiler_params=_cparams(("parallel", "parallel")),
        name="fox_prompt",
    )(q, k, v, qx, kx)


def _fox_dec_body(*refs, scale, nh, pps):
    q_ref = refs[1]
    kc_refs = refs[2:2 + pps]
    vc_refs = refs[2 + pps:2 + 2 * pps]
    ck_ref, tot_ref, lf_ref, kn_ref, vn_ref, o_ref, m_ref, l_ref, acc_ref = refs[2 + 2 * pps:]
    npos = pps * PAGE
    p = pl.program_id(1)
    npg = pl.num_programs(1)

    @pl.when(p == 0)
    def _():
        m_ref[...] = jnp.full_like(m_ref, NEG)
        l_ref[...] = jnp.zeros_like(l_ref)
        acc_ref[...] = jnp.zeros_like(acc_ref)

    q = q_ref[0] * scale
    ones = jnp.ones((HEAD, HEAD), BF16)
    sub = lax.broadcasted_iota(jnp.int32, (nh, HEAD), 0)
    lane = lax.broadcasted_iota(jnp.int32, (nh, HEAD), 1)
    own = jnp.logical_and(lax.rem(lane, nh) == sub, lane < 3 * nh)

    def head_rows(packed):
        r = packed.shape[0]
        z = jnp.where(own[None], jnp.broadcast_to(packed[:, None, :], (r, nh, HEAD)), 0.0)
        return _dot(z.reshape(r * nh, HEAD).astype(BF16), ones).reshape(r, nh, HEAD)

    cq = head_rows(_lane_pack3(jnp.broadcast_to(tot_ref[0] + lf_ref[0], (8, nh)), nh))[0]
    ck = head_rows(_lane_pack3(ck_ref[0], nh))
    kc = jnp.concatenate([r[...].reshape(PAGE, nh, HEAD) for r in kc_refs], axis=0)
    vc = jnp.concatenate([r[...].reshape(PAGE, nh, HEAD) for r in vc_refs], axis=0)
    qk = _dot((kc * q[None]).reshape(npos * nh, HEAD).astype(BF16), ones).reshape(npos, nh, HEAD)
    s = qk + (cq[None] - ck)
    m_prev = m_ref[...]
    m_new = jnp.maximum(m_prev, jnp.max(s, axis=0))
    alpha = jnp.exp(m_prev - m_new)
    pe = jnp.exp(s - m_new[None])
    l_new = alpha * l_ref[...] + jnp.sum(pe, axis=0)
    acc_new = alpha * acc_ref[...] + jnp.sum(pe * vc, axis=0)
    m_ref[...] = m_new
    l_ref[...] = l_new
    acc_ref[...] = acc_new

    @pl.when(p == npg - 1)
    def _():
        s_n = jnp.sum(kn_ref[0] * q, axis=-1, keepdims=True) + (cq - cq)
        m_fin = jnp.maximum(m_new, s_n)
        a_fin = jnp.exp(m_new - m_fin)
        p_n = jnp.exp(s_n - m_fin)
        l_fin = a_fin * l_new + p_n
        o_ref[0] = (a_fin * acc_new + p_n * vn_ref[0]) / l_fin


def fox_decode(q, k_cache, v_cache, table, ck_past, total, logf_new, k_new, v_new, n_heads):
    b, npg = table.shape
    rows = PAGE * n_heads
    pps = next(c for c in (4, 2, 1) if npg % c == 0)
    vec = pl.BlockSpec((1, n_heads, HEAD), lambda i, p, pt: (i, 0, 0))
    row = pl.BlockSpec((1, 1, n_heads), lambda i, p, pt: (i, 0, 0))
    pages = [pl.BlockSpec((rows, HEAD), lambda i, p, pt, r=r: (pt[i, p * pps + r], 0))
             for r in range(pps)]
    return pl.pallas_call(
        functools.partial(_fox_dec_body, scale=HEAD ** -0.5, nh=n_heads, pps=pps),
        grid_spec=pltpu.PrefetchScalarGridSpec(
            num_scalar_prefetch=1,
            grid=(b, npg // pps),
            in_specs=[vec] + pages + pages
            + [pl.BlockSpec((1, pps * PAGE, n_heads), lambda i, p, pt: (i, p, 0)),
               row, row, vec, vec],
            out_specs=vec,
            scratch_shapes=[pltpu.VMEM((n_heads, HEAD), F32), pltpu.VMEM((n_heads, HEAD), F32),
                            pltpu.VMEM((n_heads, HEAD), F32)]),
        out_shape=jax.ShapeDtypeStruct((b, n_heads, HEAD), F32),
        compiler_params=_cparams(("parallel", "arbitrary")),
        name="fox_decode",
    )(table, q, *([k_cache] * pps), *([v_cache] * pps), ck_past, total, logf_new, k_new, v_new)


class _Weights:
    def __init__(self, big, a_w_in, w_f, n_main):
        self.f32 = dict(big, w_in=a_w_in[:, :, :n_main])
        self.bf16 = {}
        self.n_main = n_main
        n_ab = a_w_in.shape[-1] - n_main
        self.w_ab = jnp.pad(a_w_in[:, :, n_main:], ((0, 0), (0, 0), (0, HEAD - n_ab))).astype(BF16)
        self.w_f = jnp.pad(w_f, ((0, 0), (0, HEAD - w_f.shape[1]))).astype(BF16)

    def mm(self, name, li, a, **kw):
        if (name, li) in self.bf16:
            return matmul(a, self.bf16[name, li], **kw)
        out = matmul(a, self.f32[name], layer=li, emit_w=True, **kw)
        self.bf16[name, li] = out[-1]
        return out[0] if len(out) == 2 else out[:-1]


def _trunk(x, conv_state, delta_state, past, wts, prm):
    b, l, d = x.shape
    m = b * l
    n_a = prm["a_log"].shape[0]
    n_h = prm["a_log"].shape[1]
    qk_dim = n_h * HEAD
    conv_dim = prm["a_w_conv"].shape[-1]
    decode = past is not None
    h = x.reshape(m, d)
    pend = None

    def norm(g):
        nonlocal h, pend
        if pend is None:
            return rmsnorm(h, g)
        xn, h = rmsnorm(h, g, add=pend, emit_sum=True)
        pend = None
        return xn

    def mlp(li):
        hid = wts.mm("w_up", li, norm(prm["norm_mlp_g"][li]), out_dtypes=(BF16,), relu2=True)
        return wts.mm("w_down", li, hid)

    new_conv, new_delta = [], []
    for li in range(n_a):
        xn = norm(prm["norm_mix_g"][li])
        ab = matmul(xn, wts.w_ab, layer=li)
        gb = gdn_gates(ab, prm["a_log"][li], prm["a_dt_bias"][li])
        g, beta = gb[:, :n_h], gb[:, n_h:2 * n_h]
        if decode:
            proj = wts.mm("w_in", li, xn)
            buf = conv_state[li]
            qkv = conv_decode(proj, buf, prm["a_w_conv"][li], qk_dim)
            new_conv.append(jnp.concatenate([buf[:, 1:], proj[:, None, :conv_dim]], axis=1))
            hv = lambda t: t.reshape(b, n_h, HEAD)
            rep = lambda t: jnp.broadcast_to(t[:, :, None], (b, n_h, HEAD))
            o, s_new = delta_decode(hv(qkv[:, :qk_dim]), hv(qkv[:, qk_dim:2 * qk_dim]),
                                    hv(qkv[:, 2 * qk_dim:]), hv(proj[:, conv_dim:]),
                                    rep(g), rep(beta), delta_state[li], prm["a_o_norm_g"][li])
            o = o.reshape(m, n_h * HEAD).astype(BF16)
        else:
            buf8 = jnp.pad(conv_state[li], ((0, 0), (8 - (CONV_W - 1), 0), (0, 0)))
            proj, tail = in_proj_conv(xn, wts.bf16["w_in", li], buf8, prm["a_w_conv"][li],
                                      qk_dim, l, wts.n_main)
            proj3 = proj.reshape(b, l, -1)
            new_conv.append(tail[:, 8 - (CONV_W - 1):, :])
            chunks = lambda t: t.reshape(b, l // CHUNK, CHUNK, n_h).transpose(0, 3, 1, 2)
            o, s_new = delta_prompt(proj3, proj3, chunks(g), chunks(beta), delta_state[li],
                                    prm["a_o_norm_g"][li], n_h, conv_dim)
            o = o.reshape(m, n_h * HEAD)
        new_delta.append(s_new)
        h = wts.mm("w_out", li, o, res=h)
        pend = mlp(li)

    hn = norm(prm["kv_norm_g"])
    nb = wts.f32["w_k"].shape[1]
    n_hb = nb // HEAD
    k_new, k_bf = wts.mm("w_k", None, hn, out_dtypes=(F32, BF16))
    v_new, v_bf = wts.mm("w_v", None, hn, out_dtypes=(F32, BF16))
    logf = bias_log_sigmoid(matmul(hn, wts.w_f), prm["b_f"])[:, :n_hb]
    if decode:
        k_cache, v_cache, logf_cache, table = past
        ck_past, total = paged_cumsum(logf_cache, table)
        k_cache = k_cache.reshape(-1, HEAD)
        v_cache = v_cache.reshape(-1, HEAD)
    else:
        npg = l // PAGE
        table = jnp.arange(b * npg, dtype=jnp.int32).reshape(b, npg)
        cum, _ = paged_cumsum(logf.reshape(b * npg, PAGE, n_hb), table)
        qx, kx = fox_bias_cols(cum, n_hb)
        k_bf = k_bf.reshape(b, l, nb)
        v_bf = v_bf.reshape(b, l, nb)
    for j in range(prm["b_w_q"].shape[0]):
        li = n_a + j
        xn = norm(prm["norm_mix_g"][li])
        if decode:
            q = wts.mm("w_q", j, xn)
            hv = lambda t: t.reshape(b, n_hb, HEAD)
            o = fox_decode(hv(q), k_cache, v_cache, table, ck_past, total,
                           logf.reshape(b, 1, n_hb), hv(k_new), hv(v_new), n_hb).astype(BF16)
        else:
            q = wts.mm("w_q", j, xn, out_dtypes=(BF16,), scale=HEAD ** -0.5)
            o = fox_prompt(q.reshape(b, l, nb), k_bf, v_bf, qx, kx, n_hb)
        h = wts.mm("w_o", j, o.reshape(m, nb), res=h)
        pend = mlp(li)
    y = rmsnorm(h, prm["final_norm_g"], out_dtype=F32, add=pend)
    return (y.reshape(b, l, d), jnp.stack(new_delta), jnp.stack(new_conv),
            k_new.reshape(b, l, n_hb, HEAD), v_new.reshape(b, l, n_hb, HEAD),
            logf.reshape(b, l, n_hb))


def kernel(x_prompt, x_sample, cache_k, cache_v, cache_logf, state_delta, state_conv, page_table,
           norm_mix_g, norm_mlp_g, w_up, w_down, a_w_in, a_w_conv, a_log, a_dt_bias, a_o_norm_g,
           a_w_out, kv_norm_g, w_k, w_v, w_f, b_f, b_w_q, b_w_o, final_norm_g):
    conv_dim = a_w_conv.shape[-1]
    v_dim = a_w_out.shape[1]
    wts = _Weights(dict(w_up=w_up, w_down=w_down, w_in=a_w_in, w_out=a_w_out, w_k=w_k, w_v=w_v,
                        w_q=b_w_q, w_o=b_w_o), a_w_in, w_f, conv_dim + v_dim)
    prm = dict(norm_mix_g=norm_mix_g, norm_mlp_g=norm_mlp_g, a_w_conv=a_w_conv, a_log=a_log,
               a_dt_bias=a_dt_bias, a_o_norm_g=a_o_norm_g, kv_norm_g=kv_norm_g, b_f=b_f,
               b_w_q=b_w_q, final_norm_g=final_norm_g)
    n_a = a_log.shape[0]
    n_p = x_prompt.shape[0]
    n_h = a_log.shape[1]
    zero_conv = jnp.zeros((n_a, n_p, CONV_W - 1, conv_dim), state_conv.dtype)
    zero_delta = jnp.zeros((n_a, n_p, n_h, HEAD, HEAD), state_delta.dtype)
    y_s, delta_s, conv_s, k_s, v_s, logf_s = _trunk(
        x_sample, state_conv, state_delta, (cache_k, cache_v, cache_logf, page_table), wts, prm)
    y_p, delta_p, conv_p, k_p, v_p, logf_p = _trunk(x_prompt, zero_conv, zero_delta, None, wts, prm)
    return (y_p, y_s, delta_p, conv_p, k_p, v_p, logf_p, delta_s, conv_s, k_s, v_s, logf_s)
```

```python
import functools

import jax
import jax.numpy as jnp
from jax import lax
from jax.experimental import pallas as pl
from jax.experimental.pallas import tpu as pltpu

F32 = jnp.float32
BF16 = jnp.bfloat16

EPS = 1e-6
HEAD = 128
CHUNK = 64
CONV_W = 4
PAGE = 128
NEG = -1e30
V7X_VMEM_LIMIT = 56 * 1024 * 1024


def _cparams(sem):
    return pltpu.CompilerParams(dimension_semantics=sem, vmem_limit_bytes=V7X_VMEM_LIMIT)


def _rms_body(*refs, has_add, emit_sum):
    x = refs[0][...]
    if has_add:
        x = x + refs[1][...]
    g_ref = refs[1 + has_add]
    outs = refs[2 + has_add:]
    ms = jnp.mean(x * x, axis=-1, keepdims=True)
    outs[0][...] = (x * lax.rsqrt(ms + EPS) * g_ref[...]).astype(outs[0].dtype)
    if emit_sum:
        outs[1][...] = x


def rmsnorm(x, g, out_dtype=BF16, add=None, emit_sum=False):
    m, d = x.shape
    tm = min(m, 256 if add is not None else 512)
    row = pl.BlockSpec((tm, d), lambda i: (i, 0))
    ins = [x] + ([add] if add is not None else [])
    out = pl.pallas_call(
        functools.partial(_rms_body, has_add=add is not None, emit_sum=emit_sum),
        grid=(m // tm,),
        in_specs=[row] * len(ins) + [pl.BlockSpec((1, d), lambda i: (0, 0))],
        out_specs=[row] * (1 + emit_sum),
        out_shape=[jax.ShapeDtypeStruct((m, d), out_dtype)]
        + ([jax.ShapeDtypeStruct((m, d), F32)] if emit_sum else []),
        compiler_params=_cparams(("parallel",)),
        name="rmsnorm",
    )(*ins, g.reshape(1, d))
    return tuple(out) if emit_sum else out[0]


def _mm_body(*refs, nk, relu2, scale, has_res, n_out, emit_w):
    a_ref, w_ref = refs[0], refs[1]
    pos = 2
    r_ref = None
    if has_res:
        r_ref = refs[pos]
        pos += 1
    o_refs = refs[pos:pos + n_out]
    pos += n_out
    w = w_ref[...].astype(BF16)
    if emit_w:
        refs[pos][...] = w
        pos += 1
    scr = refs[pos:]

    def finish(acc):
        if relu2:
            acc = jnp.maximum(acc, 0.0)
            acc = acc * acc
        if scale is not None:
            acc = acc * scale
        if has_res:
            acc = r_ref[...] + acc
        for o_ref in o_refs:
            o_ref[...] = acc.astype(o_ref.dtype)

    part = jnp.dot(a_ref[...], w, preferred_element_type=F32)
    if nk == 1:
        finish(part)
    else:
        acc_ref = scr[0]
        k = pl.program_id(2)

        @pl.when(k == 0)
        def _():
            acc_ref[...] = part

        @pl.when(jnp.logical_and(k > 0, k < nk - 1))
        def _():
            acc_ref[...] += part

        @pl.when(k == nk - 1)
        def _():
            finish(acc_ref[...] + part)


def matmul(a, w, out_dtypes=(F32,), res=None, relu2=False, scale=None, layer=None, n=None,
           col0=0, emit_w=False, tm=1024, tn=1024, tk=4096):
    m, kd = a.shape
    n = w.shape[-1] if n is None else n
    if kd > tk and (res is not None or emit_w):
        tk //= 2
    tm, tn, tk = min(tm, m), min(tn, n), min(tk, kd)
    nk = kd // tk
    assert not emit_w or m == tm
    if layer is None:
        w_spec = pl.BlockSpec((tk, tn), lambda i, j, k: (k, j + col0))
    else:
        w_spec = pl.BlockSpec((None, tk, tn), lambda i, j, k: (layer, k, j + col0))
    in_specs = [pl.BlockSpec((tm, tk), lambda i, j, k: (i, k)), w_spec]
    args = [a, w]
    if res is not None:
        in_specs.append(pl.BlockSpec((tm, tn), lambda i, j, k: (i, j)))
        args.append(res)
    out_specs = [pl.BlockSpec((tm, tn), lambda i, j, k: (i, j)) for _ in out_dtypes]
    out_shape = [jax.ShapeDtypeStruct((m, n), dt) for dt in out_dtypes]
    if emit_w:
        out_specs.append(pl.BlockSpec((tk, tn), lambda i, j, k: (k, j)))
        out_shape.append(jax.ShapeDtypeStruct((kd, n), BF16))
    out = pl.pallas_call(
        functools.partial(_mm_body, nk=nk, relu2=relu2, scale=scale, has_res=res is not None,
                          n_out=len(out_dtypes), emit_w=emit_w),
        grid=(m // tm, n // tn, nk),
        in_specs=in_specs,
        out_specs=out_specs,
        out_shape=out_shape,
        scratch_shapes=[pltpu.VMEM((tm, tn), F32)] if nk > 1 else [],
        compiler_params=_cparams(("parallel", "parallel", "arbitrary")),
        name="matmul",
    )(*args)
    return out[0] if len(out) == 1 else tuple(out)


def _softplus(x):
    return jnp.maximum(x, 0.0) + jnp.log1p(jnp.exp(-jnp.abs(x)))


def _gates_body(x_ref, alog_ref, dt_ref, o_ref, *, nh):
    x = x_ref[...]
    lane = lax.broadcasted_iota(jnp.int32, x.shape, 1)
    g = -jnp.exp(alog_ref[...]) * _softplus(x + dt_ref[...])
    beta = jax.nn.sigmoid(x)
    o_ref[...] = jnp.where(lane < nh, g, beta)


def gdn_gates(ab, a_log, dt_bias):
    m, w = ab.shape
    nh = a_log.shape[0]
    tm = min(m, 1024)
    pad = lambda v: jnp.pad(v.astype(F32), (0, w - nh)).reshape(1, w)
    return pl.pallas_call(
        functools.partial(_gates_body, nh=nh),
        grid=(m // tm,),
        in_specs=[pl.BlockSpec((tm, w), lambda i: (i, 0)),
                  pl.BlockSpec((1, w), lambda i: (0, 0)),
                  pl.BlockSpec((1, w), lambda i: (0, 0))],
        out_specs=pl.BlockSpec((tm, w), lambda i: (i, 0)),
        out_shape=jax.ShapeDtypeStruct((m, w), F32),
        compiler_params=_cparams(("parallel",)),
        name="gdn_gates",
    )(ab, pad(a_log), pad(dt_bias))


def _logsig_body(x_ref, b_ref, o_ref):
    y = x_ref[...] + b_ref[...]
    o_ref[...] = -_softplus(-y)


def bias_log_sigmoid(x, b):
    m, w = x.shape
    tm = min(m, 1024)
    return pl.pallas_call(
        _logsig_body,
        grid=(m // tm,),
        in_specs=[pl.BlockSpec((tm, w), lambda i: (i, 0)),
                  pl.BlockSpec((1, w), lambda i: (0, 0))],
        out_specs=pl.BlockSpec((tm, w), lambda i: (i, 0)),
        out_shape=jax.ShapeDtypeStruct((m, w), F32),
        compiler_params=_cparams(("parallel",)),
        name="bias_log_sigmoid",
    )(x, jnp.pad(b.astype(F32), (0, w - b.shape[0])).reshape(1, w))


def _l2_heads(y, o_ref, idx, qscale):
    for j in range(y.shape[-1] // HEAD):
        ys = y[..., j * HEAD:(j + 1) * HEAD]
        ss = jnp.sum(ys * ys, axis=-1, keepdims=True)
        o_ref[idx + (slice(j * HEAD, (j + 1) * HEAD),)] = ys * lax.rsqrt(ss + EPS) * qscale


def _in_conv_body(a_ref, w_ref, wc_ref, buf_ref, o_ref, tail_ref, ext_ref, halo_ref,
                  *, tm, n_q, n_conv, tiles_per_seq):
    i = pl.program_id(0)
    j = pl.program_id(1)
    jc = jnp.minimum(j, n_conv)
    first = lax.rem(i, tiles_per_seq) == 0
    nsub, _, sub = ext_ref.shape[0], ext_ref.shape[1], ext_ref.shape[2]

    @pl.when(first)
    def _():
        for s in range(nsub):
            ext_ref[s, 0:8, :] = buf_ref[0, :, s * sub:(s + 1) * sub]

    @pl.when(jnp.logical_not(first))
    def _():
        for s in range(nsub):
            ext_ref[s, 0:8, :] = halo_ref[jc, :, s * sub:(s + 1) * sub]

    is_conv = j < n_conv
    is_qk = j < 2 * n_q
    qscale = jnp.where(j < n_q, HEAD ** -0.5, 1.0).astype(F32)

    def finish(s, acc):
        cs = slice(s * sub, (s + 1) * sub)
        ext_ref[s, 8:8 + tm, :] = acc
        wc = wc_ref[:, cs]
        cv = ext_ref[s, 5:5 + tm, :] * wc[0:1]
        cv = cv + ext_ref[s, 6:6 + tm, :] * wc[1:2]
        cv = cv + ext_ref[s, 7:7 + tm, :] * wc[2:3]
        cv = cv + acc * wc[3:4]
        y = cv * jax.nn.sigmoid(cv)
        tail = acc[tm - 8:tm]
        halo_ref[jc, :, cs] = tail
        tail_ref[0, :, cs] = tail
        for hh in range(sub // HEAD):
            hs = slice(hh * HEAD, (hh + 1) * HEAD)
            ys = y[:, hs]
            ss = jnp.sum(ys * ys, axis=-1, keepdims=True)
            inv = jnp.where(is_qk, lax.rsqrt(ss + EPS), 1.0)
            o_ref[:, s * sub + hh * HEAD:s * sub + (hh + 1) * HEAD] = jnp.where(
                is_conv, ys * inv * qscale, acc[:, hs])

    pending = None
    for s in range(nsub):
        acc = jnp.dot(a_ref[...], w_ref[:, s * sub:(s + 1) * sub], preferred_element_type=F32)
        if pending is not None:
            finish(*pending)
        pending = (s, acc)
    finish(*pending)


def in_proj_conv(xn, w, buf8, w_conv, qk_dim, seq_len, n_main, tm=1024, tn=1024):
    m, kd = xn.shape
    cd = w_conv.shape[1]
    tm, tn = min(tm, seq_len), min(tn, qk_dim)
    sub = min(tn, 256)
    tps = seq_len // tm
    n_conv = cd // tn
    cj = lambda j: jnp.minimum(j, n_conv - 1)
    out, tail = pl.pallas_call(
        functools.partial(_in_conv_body, tm=tm, n_q=qk_dim // tn, n_conv=n_conv,
                          tiles_per_seq=tps),
        grid=(m // tm, n_main // tn),
        in_specs=[pl.BlockSpec((tm, kd), lambda i, j: (i, 0)),
                  pl.BlockSpec((kd, tn), lambda i, j: (0, j)),
                  pl.BlockSpec((CONV_W, tn), lambda i, j: (0, cj(j))),
                  pl.BlockSpec((1, 8, tn), lambda i, j: (i // tps, 0, cj(j)))],
        out_specs=[pl.BlockSpec((tm, tn), lambda i, j: (i, j)),
                   pl.BlockSpec((1, 8, tn), lambda i, j: (i, 0, jnp.minimum(j, n_conv)))],
        out_shape=[jax.ShapeDtypeStruct((m, n_main), F32),
                   jax.ShapeDtypeStruct((m // tm, 8, cd + tn), F32)],
        scratch_shapes=[pltpu.VMEM((tn // sub, tm + 8, sub), F32),
                        pltpu.VMEM((n_conv + 1, 8, tn), F32)],
        compiler_params=_cparams(("arbitrary", "arbitrary")),
        name="in_proj_conv",
    )(xn, w, w_conv, buf8)
    return out, tail.reshape(m // seq_len, tps, 8, cd + tn)[:, tps - 1, :, :cd]


def _conv_dec_body(u_ref, buf_ref, w_ref, o_ref, *, n_q_tiles):
    c = pl.program_id(0)
    w = w_ref[...]
    acc = buf_ref[:, 0, :] * w[0:1]
    acc = acc + buf_ref[:, 1, :] * w[1:2]
    acc = acc + buf_ref[:, 2, :] * w[2:3]
    acc = acc + u_ref[...] * w[3:4]
    y = acc * jax.nn.sigmoid(acc)

    @pl.when(c < 2 * n_q_tiles)
    def _():
        qscale = jnp.where(c < n_q_tiles, HEAD ** -0.5, 1.0).astype(F32)
        _l2_heads(y, o_ref, (slice(None),), qscale)

    @pl.when(c >= 2 * n_q_tiles)
    def _():
        o_ref[...] = y


def conv_decode(proj, buf, w_conv, qk_dim, tc=512):
    b = proj.shape[0]
    cd = w_conv.shape[1]
    tc = min(tc, qk_dim)
    return pl.pallas_call(
        functools.partial(_conv_dec_body, n_q_tiles=qk_dim // tc),
        grid=(cd // tc,),
        in_specs=[pl.BlockSpec((b, tc), lambda c: (0, c)),
                  pl.BlockSpec((b, CONV_W - 1, tc), lambda c: (0, 0, c)),
                  pl.BlockSpec((CONV_W, tc), lambda c: (0, c))],
        out_specs=pl.BlockSpec((b, tc), lambda c: (0, c)),
        out_shape=jax.ShapeDtypeStruct((b, cd), F32),
        compiler_params=_cparams(("parallel",)),
        name="conv_decode",
    )(proj, buf, w_conv)


def _split(x):
    hi = x.astype(BF16)
    lo = (x - hi.astype(F32)).astype(BF16)
    return hi, lo


def _dot(a, b, dims=(((1,), (0,)), ((), ()))):
    return lax.dot_general(a, b, dims, preferred_element_type=F32)


_NT = (((1,), (1,)), ((), ()))
_TN = (((0,), (0,)), ((), ()))


def _dot1(a, b, dims=(((1,), (0,)), ((), ()))):
    return _dot(a.astype(BF16), b.astype(BF16), dims)


def _dot3(a, b, dims=(((1,), (0,)), ((), ()))):
    ah, al = _split(a)
    bh, bl = _split(b)
    return _dot(ah, bh, dims) + (_dot(ah, bl, dims) + _dot(al, bh, dims))


_BMM = (((2,), (1,)), ((0,), (0,)))
_BMM_NT = (((2,), (2,)), ((0,), (0,)))
_BMM_TN = (((1,), (1,)), ((0,), (0,)))
INV_PASSES = 1


def _bdot(a, b, dims=_BMM, passes=1):
    if passes == 1:
        return _dot(a.astype(BF16), b.astype(BF16), dims)
    ah, al = _split(a)
    bh, bl = _split(b)
    return _dot(ah, bh, dims) + (_dot(ah, bl, dims) + _dot(al, bh, dims))


def _split3(x):
    hi = x.astype(BF16)
    r1 = x - hi.astype(F32)
    mid = r1.astype(BF16)
    lo = (r1 - mid.astype(F32)).astype(BF16)
    return hi, mid, lo


def _unit_lower_inverse(a_strict, ii, jj):
    c = a_strict.shape[-1]
    eye = (ii == jj).astype(F32)
    m = None
    s = 1
    k = 0
    while s < c:
        off = jnp.logical_and((ii >> (k + 1)) == (jj >> (k + 1)),
                              jnp.logical_and(((ii >> k) & 1) == 1, ((jj >> k) & 1) == 0))
        a_off = jnp.where(off, a_strict, 0.0)
        if m is None:
            m = eye - a_off
        else:
            m = m - _bdot(m, _bdot(a_off, m, passes=INV_PASSES), passes=INV_PASSES)
        s *= 2
        k += 1
    return m


def _delta_body(q_ref, k_ref, v_ref, z_ref, g_ref, b_ref, s0_ref, gn_ref, o_ref, sf_ref, s_ref,
                *, hb, tl):
    t = pl.program_id(2)
    nt = pl.num_programs(2)
    c = CHUNK
    npc = tl // c

    @pl.when(t == 0)
    def _():
        s_ref[...] = s0_ref[0]

    ii = lax.broadcasted_iota(jnp.int32, (c, c), 0)
    jj = lax.broadcasted_iota(jnp.int32, (c, c), 1)
    tril = ii >= jj
    stril = ii > jj
    tril_b = tril.astype(BF16)
    triu_b = (ii <= jj).astype(BF16)
    eye_b = (ii == jj).astype(BF16)
    gn = gn_ref[...]

    row0 = pl.multiple_of(t * npc, npc)
    g_rows = jnp.concatenate([g_ref[0, hh, pl.ds(row0, npc), :] for hh in range(hb)], axis=0)
    b_rows = jnp.concatenate([b_ref[0, hh, pl.ds(row0, npc), :] for hh in range(hb)], axis=0)
    g3 = _split3(g_rows)
    b3 = _split3(b_rows)
    dcum_rows = _dot(g3[0], triu_b) + (_dot(g3[1], triu_b) + _dot(g3[2], triu_b))
    dcum_cols = _dot(tril_b, g3[0], _NT) + (_dot(tril_b, g3[1], _NT) + _dot(tril_b, g3[2], _NT))
    b_cols = _dot(eye_b, b3[0], _NT) + (_dot(eye_b, b3[1], _NT) + _dot(eye_b, b3[2], _NT))

    ks, qs, decs, rhss, dcs, bcs = [], [], [], [], [], []
    for j in range(npc):
        rs = slice(j * c, (j + 1) * c)
        for hh in range(hb):
            hs = slice(hh * HEAD, (hh + 1) * HEAD)
            gi = hh * npc + j
            k = k_ref[0, rs, hs]
            d_col = dcum_cols[:, gi:gi + 1]
            b_col = b_cols[:, gi:gi + 1]
            diff = d_col - dcum_rows[gi:gi + 1, :]
            decs.append(jnp.where(tril, jnp.exp(jnp.where(tril, diff, 0.0)), 0.0))
            rhss.append(jnp.concatenate([v_ref[0, rs, hs] * b_col,
                                         k * (b_col * jnp.exp(d_col))], axis=1))
            ks.append(k)
            qs.append(q_ref[0, rs, hs])
            dcs.append(d_col)
            bcs.append(b_col)
    kb = jnp.stack(ks)
    qb = jnp.stack(qs)
    dec = jnp.stack(decs)
    d_colb = jnp.stack(dcs)
    d_lastb = d_colb[:, c - 1:c, :]
    a_mat = jnp.stack(bcs) * _bdot(kb, kb, _BMM_NT, passes=INV_PASSES) * jnp.where(stril, dec, 0.0)
    sol = _bdot(_unit_lower_inverse(a_mat, ii, jj), jnp.stack(rhss), passes=INV_PASSES)
    attn = _bdot(qb, kb, _BMM_NT) * dec
    wq = jnp.concatenate([sol[:, :, HEAD:], qb * jnp.exp(d_colb)], axis=1)
    k_dec = kb * jnp.exp(d_lastb - d_colb)
    g_last = jnp.exp(d_lastb)

    s_mat = s_ref[...]
    for j in range(npc):
        ps = slice(j * hb, (j + 1) * hb)
        rs = slice(j * c, (j + 1) * c)
        wq_s = _bdot(wq[ps], s_mat)
        v_new = sol[ps, :, :HEAD] - wq_s[:, :c]
        o = wq_s[:, c:] + _bdot(attn[ps], v_new)
        s_mat = s_mat * g_last[ps] + _bdot(k_dec[ps], v_new, _BMM_TN)
        ms = jnp.mean(o * o, axis=-1, keepdims=True)
        on = o * lax.rsqrt(ms + EPS) * gn
        for hh in range(hb):
            hs = slice(hh * HEAD, (hh + 1) * HEAD)
            z = z_ref[0, rs, hs]
            o_ref[0, rs, hs] = (on[hh] * (z * jax.nn.sigmoid(z))).astype(o_ref.dtype)
    s_ref[...] = s_mat

    @pl.when(t == nt - 1)
    def _():
        sf_ref[0] = s_ref[...]


def delta_prompt(qkv, proj, g, beta, s0, gnorm, n_heads, cd, hb=8, tl=512):
    b, l, _ = qkv.shape
    tl, hb = min(tl, l), min(hb, n_heads)
    nhg = n_heads // hb
    wb = hb * HEAD
    nc = l // CHUNK
    z_off = cd // wb
    return pl.pallas_call(
        functools.partial(_delta_body, hb=hb, tl=tl),
        grid=(b, nhg, l // tl),
        in_specs=[pl.BlockSpec((1, tl, wb), lambda i, h, t: (i, t, h)),
                  pl.BlockSpec((1, tl, wb), lambda i, h, t: (i, t, nhg + h)),
                  pl.BlockSpec((1, tl, wb), lambda i, h, t: (i, t, 2 * nhg + h)),
                  pl.BlockSpec((1, tl, wb), lambda i, h, t: (i, t, z_off + h)),
                  pl.BlockSpec((1, hb, nc, CHUNK), lambda i, h, t: (i, h, 0, 0)),
                  pl.BlockSpec((1, hb, nc, CHUNK), lambda i, h, t: (i, h, 0, 0)),
                  pl.BlockSpec((1, hb, HEAD, HEAD), lambda i, h, t: (i, h, 0, 0)),
                  pl.BlockSpec((1, HEAD), lambda i, h, t: (0, 0))],
        out_specs=[pl.BlockSpec((1, tl, wb), lambda i, h, t: (i, t, h)),
                   pl.BlockSpec((1, hb, HEAD, HEAD), lambda i, h, t: (i, h, 0, 0))],
        out_shape=[jax.ShapeDtypeStruct((b, l, n_heads * HEAD), BF16),
                   jax.ShapeDtypeStruct((b, n_heads, HEAD, HEAD), F32)],
        scratch_shapes=[pltpu.VMEM((hb, HEAD, HEAD), F32)],
        compiler_params=_cparams(("parallel", "parallel", "arbitrary")),
        name="delta_prompt",
    )(qkv, qkv, qkv, proj, g, beta, s0, gnorm.reshape(1, HEAD))


def _delta_dec_body(q_ref, k_ref, v_ref, z_ref, g_ref, b_ref, s0_ref, gn_ref, o_ref, sf_ref, *, nh):
    ii = lax.broadcasted_iota(jnp.int32, (HEAD, HEAD), 0)
    jj = lax.broadcasted_iota(jnp.int32, (HEAD, HEAD), 1)
    eye = ii == jj
    gn = gn_ref[...]

    def col_of(row):
        return jnp.sum(jnp.where(eye, jnp.broadcast_to(row, (HEAD, HEAD)), 0.0), axis=1, keepdims=True)

    def head(h, carry):
        q = q_ref[0, pl.ds(h, 1), :]
        k = k_ref[0, pl.ds(h, 1), :]
        v = v_ref[0, pl.ds(h, 1), :]
        z = z_ref[0, pl.ds(h, 1), :]
        eg = jnp.exp(g_ref[0, pl.ds(h, 1), :])
        beta = b_ref[0, pl.ds(h, 1), :]
        s_mat = s0_ref[0, h]
        k_col = col_of(k)
        ks = jnp.sum(k_col * s_mat, axis=0, keepdims=True)
        qs = jnp.sum(col_of(q) * s_mat, axis=0, keepdims=True)
        v_new = v * beta - (beta * eg) * ks
        qk = jnp.sum(q * k, axis=-1, keepdims=True)
        o = eg * qs + qk * v_new
        sf_ref[0, h] = s_mat * eg + k_col * v_new
        ms = jnp.mean(o * o, axis=-1, keepdims=True)
        on = o * lax.rsqrt(ms + EPS) * gn
        o_ref[0, pl.ds(h, 1), :] = (on * (z * jax.nn.sigmoid(z))).astype(o_ref.dtype)
        return carry

    lax.fori_loop(0, nh, head, 0)


def delta_decode(q, k, v, z, g_rep, b_rep, s0, gnorm):
    b, nh, _ = q.shape
    vec = pl.BlockSpec((1, nh, HEAD), lambda i: (i, 0, 0))
    st = pl.BlockSpec((1, nh, HEAD, HEAD), lambda i: (i, 0, 0, 0))
    return pl.pallas_call(
        functools.partial(_delta_dec_body, nh=nh),
        grid=(b,),
        in_specs=[vec, vec, vec, vec, vec, vec, st, pl.BlockSpec((1, HEAD), lambda i: (0, 0))],
        out_specs=[vec, st],
        out_shape=[jax.ShapeDtypeStruct((b, nh, HEAD), F32),
                   jax.ShapeDtypeStruct((b, nh, HEAD, HEAD), F32)],
        compiler_params=_cparams(("parallel",)),
        name="delta_decode",
    )(q, k, v, z, g_rep, b_rep, s0, gnorm.reshape(1, HEAD))


def _cumsum_body(*refs, pps):
    x_refs = refs[1:1 + pps]
    o_ref, tot_ref, carry_ref = refs[1 + pps:]
    p = pl.program_id(1)
    npg = pl.num_programs(1)

    @pl.when(p == 0)
    def _():
        carry_ref[...] = jnp.zeros_like(carry_ref)

    n = x_refs[0].shape[1]
    tri = (lax.broadcasted_iota(jnp.int32, (n, n), 0)
           >= lax.broadcasted_iota(jnp.int32, (n, n), 1)).astype(BF16)
    carry = carry_ref[...]
    for r, x_ref in enumerate(x_refs):
        hi, mid, lo = _split3(x_ref[0])
        out = (_dot(tri, hi) + (_dot(tri, mid) + _dot(tri, lo))) + carry
        o_ref[0, r * n:(r + 1) * n, :] = out
        carry = out[n - 1:n, :]
    carry_ref[...] = carry

    @pl.when(p == npg - 1)
    def _():
        tot_ref[0] = carry


def paged_cumsum(pages, table):
    b, npg = table.shape
    _, pg, h = pages.shape
    pps = next(c for c in (8, 4, 2, 1) if npg % c == 0)
    page_spec = lambda r: pl.BlockSpec((1, pg, h), lambda i, p, pt: (pt[i, p * pps + r], 0, 0))
    return pl.pallas_call(
        functools.partial(_cumsum_body, pps=pps),
        grid_spec=pltpu.PrefetchScalarGridSpec(
            num_scalar_prefetch=1,
            grid=(b, npg // pps),
            in_specs=[page_spec(r) for r in range(pps)],
            out_specs=[pl.BlockSpec((1, pps * pg, h), lambda i, p, pt: (i, p, 0)),
                       pl.BlockSpec((1, 1, h), lambda i, p, pt: (i, 0, 0))],
            scratch_shapes=[pltpu.VMEM((1, h), F32)]),
        out_shape=[jax.ShapeDtypeStruct((b, npg * pg, h), F32),
                   jax.ShapeDtypeStruct((b, 1, h), F32)],
        compiler_params=_cparams(("parallel", "arbitrary")),
        name="paged_cumsum",
    )(table, *([pages] * pps))


def _lane_pack3(x, nh):
    pieces = _split3(x)
    row = lax.broadcasted_iota(jnp.int32, (nh, HEAD), 0)
    lane = lax.broadcasted_iota(jnp.int32, (nh, HEAD), 1)
    out = None
    for gi, pc in enumerate(pieces):
        part = _dot(pc, (lane == gi * nh + row).astype(BF16))
        out = part if out is None else out + part
    return out


def _bias_cols_body(cum_ref, qx_ref, kx_ref, *, nh):
    hg = qx_ref.shape[1]
    packed = _lane_pack3(cum_ref[0], nh).astype(BF16)
    r = lax.broadcasted_iota(jnp.int32, (HEAD, HEAD), 0)
    j = lax.broadcasted_iota(jnp.int32, (HEAD, HEAD), 1)
    piece = lax.div(r, nh)
    lane = lax.broadcasted_iota(jnp.int32, (1, HEAD), 1)
    ones_k = jnp.logical_and(lane >= 3, lane < 6).astype(F32)
    ones_q = (lane < 3).astype(F32)
    for hh in range(hg):
        h = pl.program_id(1) * hg + hh
        mine = jnp.logical_and(lax.rem(r, nh) == h, piece < 3)
        to_k = jnp.logical_and(mine, j == piece).astype(BF16)
        to_q = jnp.logical_and(mine, j == piece + 3).astype(BF16)
        kx_ref[0, hh] = (ones_k - _dot(packed, to_k)).astype(BF16)
        qx_ref[0, hh] = (ones_q + _dot(packed, to_q)).astype(BF16)


def fox_bias_cols(cum, n_heads):
    b, l, _ = cum.shape
    hg = min(8, n_heads)
    out = pl.BlockSpec((1, hg, l, HEAD), lambda i, h: (i, h, 0, 0))
    return pl.pallas_call(
        functools.partial(_bias_cols_body, nh=n_heads),
        grid=(b, n_heads // hg),
        in_specs=[pl.BlockSpec((1, l, n_heads), lambda i, h: (i, 0, 0))],
        out_specs=[out, out],
        out_shape=[jax.ShapeDtypeStruct((b, n_heads, l, HEAD), BF16)] * 2,
        compiler_params=_cparams(("parallel", "parallel")),
        name="fox_bias_cols",
    )(cum)


def _flash_body(q_ref, k_ref, v_ref, qx_ref, kx_ref, o_ref, *, tq):
    nq = q_ref.shape[1] // tq
    causal = (lax.broadcasted_iota(jnp.int32, (tq, tq), 0)
              >= lax.broadcasted_iota(jnp.int32, (tq, tq), 1))
    for qi in range(nq):
        rq = slice(qi * tq, (qi + 1) * tq)
        qa = jnp.concatenate([q_ref[0, rq, :], qx_ref[0, 0, rq, :]], axis=1)
        m = jnp.full((tq, 1), NEG, F32)
        lsum = jnp.zeros((tq, 1), F32)
        acc = jnp.zeros((tq, HEAD), F32)
        for ki in range(qi + 1):
            rk = slice(ki * tq, (ki + 1) * tq)
            ka = jnp.concatenate([k_ref[0, rk, :], kx_ref[0, 0, rk, :]], axis=1)
            s = _dot(qa, ka, _NT)
            if ki == qi:
                s = jnp.where(causal, s, NEG)
            m_new = jnp.maximum(m, jnp.max(s, axis=1, keepdims=True))
            alpha = jnp.exp(m - m_new)
            p = jnp.exp(s - m_new)
            lsum = alpha * lsum + jnp.sum(p, axis=1, keepdims=True)
            acc = alpha * acc + _dot(p.astype(BF16), v_ref[0, rk, :])
            m = m_new
        o_ref[0, rq, :] = (acc / lsum).astype(o_ref.dtype)


def fox_prompt(q, k, v, qx, kx, n_heads, tq=512):
    b, l, _ = q.shape
    tq = min(tq, l)
    tok = pl.BlockSpec((1, l, HEAD), lambda i, h: (i, 0, h))
    col = pl.BlockSpec((1, 1, l, HEAD), lambda i, h: (i, h, 0, 0))
    return pl.pallas_call(
        functools.partial(_flash_body, tq=tq),
        grid=(b, n_heads),
        in_specs=[tok, tok, tok, col, col],
        out_specs=tok,
        out_shape=jax.ShapeDtypeStruct(q.shape, BF16),
        compiler_params=_cparams(("parallel", "parallel")),
        name="fox_prompt",
    )(q, k, v, qx, kx)


def _fox_dec_body(*refs, scale, nh, pps):
    q_ref = refs[1]
    kc_refs = refs[2:2 + pps]
    vc_refs = refs[2 + pps:2 + 2 * pps]
    ck_ref, tot_ref, lf_ref, kn_ref, vn_ref, o_ref, m_ref, l_ref, acc_ref = refs[2 + 2 * pps:]
    npos = pps * PAGE
    p = pl.program_id(1)
    npg = pl.num_programs(1)

    @pl.when(p == 0)
    def _():
        m_ref[...] = jnp.full_like(m_ref, NEG)
        l_ref[...] = jnp.zeros_like(l_ref)
        acc_ref[...] = jnp.zeros_like(acc_ref)

    q = q_ref[0] * scale
    ones = jnp.ones((HEAD, HEAD), BF16)
    sub = lax.broadcasted_iota(jnp.int32, (nh, HEAD), 0)
    lane = lax.broadcasted_iota(jnp.int32, (nh, HEAD), 1)
    own = jnp.logical_and(lax.rem(lane, nh) == sub, lane < 3 * nh)

    def head_rows(packed):
        r = packed.shape[0]
        z = jnp.where(own[None], jnp.broadcast_to(packed[:, None, :], (r, nh, HEAD)), 0.0)
        return _dot(z.reshape(r * nh, HEAD).astype(BF16), ones).reshape(r, nh, HEAD)

    cq = head_rows(_lane_pack3(jnp.broadcast_to(tot_ref[0] + lf_ref[0], (8, nh)), nh))[0]
    ck = head_rows(_lane_pack3(ck_ref[0], nh))
    kc = jnp.concatenate([r[...].reshape(PAGE, nh, HEAD) for r in kc_refs], axis=0)
    vc = jnp.concatenate([r[...].reshape(PAGE, nh, HEAD) for r in vc_refs], axis=0)
    qk = _dot((kc * q[None]).reshape(npos * nh, HEAD).astype(BF16), ones).reshape(npos, nh, HEAD)
    s = qk + (cq[None] - ck)
    m_prev = m_ref[...]
    m_new = jnp.maximum(m_prev, jnp.max(s, axis=0))
    alpha = jnp.exp(m_prev - m_new)
    pe = jnp.exp(s - m_new[None])
    l_new = alpha * l_ref[...] + jnp.sum(pe, axis=0)
    acc_new = alpha * acc_ref[...] + jnp.sum(pe * vc, axis=0)
    m_ref[...] = m_new
    l_ref[...] = l_new
    acc_ref[...] = acc_new

    @pl.when(p == npg - 1)
    def _():
        s_n = jnp.sum(kn_ref[0] * q, axis=-1, keepdims=True) + (cq - cq)
        m_fin = jnp.maximum(m_new, s_n)
        a_fin = jnp.exp(m_new - m_fin)
        p_n = jnp.exp(s_n - m_fin)
        l_fin = a_fin * l_new + p_n
        o_ref[0] = (a_fin * acc_new + p_n * vn_ref[0]) / l_fin


def fox_decode(q, k_cache, v_cache, table, ck_past, total, logf_new, k_new, v_new, n_heads):
    b, npg = table.shape
    rows = PAGE * n_heads
    pps = next(c for c in (4, 2, 1) if npg % c == 0)
    vec = pl.BlockSpec((1, n_heads, HEAD), lambda i, p, pt: (i, 0, 0))
    row = pl.BlockSpec((1, 1, n_heads), lambda i, p, pt: (i, 0, 0))
    pages = [pl.BlockSpec((rows, HEAD), lambda i, p, pt, r=r: (pt[i, p * pps + r], 0))
             for r in range(pps)]
    return pl.pallas_call(
        functools.partial(_fox_dec_body, scale=HEAD ** -0.5, nh=n_heads, pps=pps),
        grid_spec=pltpu.PrefetchScalarGridSpec(
            num_scalar_prefetch=1,
            grid=(b, npg // pps),
            in_specs=[vec] + pages + pages
            + [pl.BlockSpec((1, pps * PAGE, n_heads), lambda i, p, pt: (i, p, 0)),
               row, row, vec, vec],
            out_specs=vec,
            scratch_shapes=[pltpu.VMEM((n_heads, HEAD), F32), pltpu.VMEM((n_heads, HEAD), F32),
                            pltpu.VMEM((n_heads, HEAD), F32)]),
        out_shape=jax.ShapeDtypeStruct((b, n_heads, HEAD), F32),
        compiler_params=_cparams(("parallel", "arbitrary")),
        name="fox_decode",
    )(table, q, *([k_cache] * pps), *([v_cache] * pps), ck_past, total, logf_new, k_new, v_new)


class _Weights:
    def __init__(self, big, w_f, n_main):
        self.f32 = big
        self.bf16 = {}
        self.n_main = n_main
        self.w_f = jnp.pad(w_f, ((0, 0), (0, HEAD - w_f.shape[1]))).astype(BF16)

    def gates(self, li, a):
        assert self.n_main % HEAD == 0 and self.f32["w_in"].shape[-1] - self.n_main <= HEAD
        return matmul(a, self.f32["w_in"], layer=li, n=HEAD, col0=self.n_main // HEAD)

    def mm(self, name, li, a, **kw):
        n = self.n_main if name == "w_in" else None
        if (name, li) in self.bf16:
            return matmul(a, self.bf16[name, li], **kw)
        out = matmul(a, self.f32[name], layer=li, n=n, emit_w=True, **kw)
        self.bf16[name, li] = out[-1]
        return out[0] if len(out) == 2 else out[:-1]


def _trunk(x, conv_state, delta_state, past, wts, prm):
    b, l, d = x.shape
    m = b * l
    n_a = prm["a_log"].shape[0]
    n_h = prm["a_log"].shape[1]
    qk_dim = n_h * HEAD
    conv_dim = prm["a_w_conv"].shape[-1]
    decode = past is not None
    h = x.reshape(m, d)
    pend = None

    def norm(g):
        nonlocal h, pend
        if pend is None:
            return rmsnorm(h, g)
        xn, h = rmsnorm(h, g, add=pend, emit_sum=True)
        pend = None
        return xn

    def mlp(li):
        hid = wts.mm("w_up", li, norm(prm["norm_mlp_g"][li]), out_dtypes=(BF16,), relu2=True)
        return wts.mm("w_down", li, hid)

    new_conv, new_delta = [], []
    for li in range(n_a):
        xn = norm(prm["norm_mix_g"][li])
        ab = wts.gates(li, xn)
        gb = gdn_gates(ab, prm["a_log"][li], prm["a_dt_bias"][li])
        g, beta = gb[:, :n_h], gb[:, n_h:2 * n_h]
        if decode:
            proj = wts.mm("w_in", li, xn)
            buf = conv_state[li]
            qkv = conv_decode(proj, buf, prm["a_w_conv"][li], qk_dim)
            new_conv.append(jnp.concatenate([buf[:, 1:], proj[:, None, :conv_dim]], axis=1))
            hv = lambda t: t.reshape(b, n_h, HEAD)
            rep = lambda t: jnp.broadcast_to(t[:, :, None], (b, n_h, HEAD))
            o, s_new = delta_decode(hv(qkv[:, :qk_dim]), hv(qkv[:, qk_dim:2 * qk_dim]),
                                    hv(qkv[:, 2 * qk_dim:]), hv(proj[:, conv_dim:]),
                                    rep(g), rep(beta), delta_state[li], prm["a_o_norm_g"][li])
            o = o.reshape(m, n_h * HEAD).astype(BF16)
        else:
            buf8 = jnp.pad(conv_state[li], ((0, 0), (8 - (CONV_W - 1), 0), (0, 0)))
            proj, tail = in_proj_conv(xn, wts.bf16["w_in", li], buf8, prm["a_w_conv"][li],
                                      qk_dim, l, wts.n_main)
            proj3 = proj.reshape(b, l, -1)
            new_conv.append(tail[:, 8 - (CONV_W - 1):, :])
            chunks = lambda t: t.reshape(b, l // CHUNK, CHUNK, n_h).transpose(0, 3, 1, 2)
            o, s_new = delta_prompt(proj3, proj3, chunks(g), chunks(beta), delta_state[li],
                                    prm["a_o_norm_g"][li], n_h, conv_dim)
            o = o.reshape(m, n_h * HEAD)
        new_delta.append(s_new)
        h = wts.mm("w_out", li, o, res=h)
        pend = mlp(li)

    hn = norm(prm["kv_norm_g"])
    nb = wts.f32["w_k"].shape[1]
    n_hb = nb // HEAD
    k_new, k_bf = wts.mm("w_k", None, hn, out_dtypes=(F32, BF16))
    v_new, v_bf = wts.mm("w_v", None, hn, out_dtypes=(F32, BF16))
    logf = bias_log_sigmoid(matmul(hn, wts.w_f), prm["b_f"])[:, :n_hb]
    if decode:
        k_cache, v_cache, logf_cache, table = past
        ck_past, total = paged_cumsum(logf_cache, table)
        k_cache = k_cache.reshape(-1, HEAD)
        v_cache = v_cache.reshape(-1, HEAD)
    else:
        npg = l // PAGE
        table = jnp.arange(b * npg, dtype=jnp.int32).reshape(b, npg)
        cum, _ = paged_cumsum(logf.reshape(b * npg, PAGE, n_hb), table)
        qx, kx = fox_bias_cols(cum, n_hb)
        k_bf = k_bf.reshape(b, l, nb)
        v_bf = v_bf.reshape(b, l, nb)
    for j in range(prm["b_w_q"].shape[0]):
        li = n_a + j
        xn = norm(prm["norm_mix_g"][li])
        if decode:
            q = wts.mm("w_q", j, xn)
            hv = lambda t: t.reshape(b, n_hb, HEAD)
            o = fox_decode(hv(q), k_cache, v_cache, table, ck_past, total,
                           logf.reshape(b, 1, n_hb), hv(k_new), hv(v_new), n_hb).astype(BF16)
        else:
            q = wts.mm("w_q", j, xn, out_dtypes=(BF16,), scale=HEAD ** -0.5)
            o = fox_prompt(q.reshape(b, l, nb), k_bf, v_bf, qx, kx, n_hb)
        h = wts.mm("w_o", j, o.reshape(m, nb), res=h)
        pend = mlp(li)
    y = rmsnorm(h, prm["final_norm_g"], out_dtype=F32, add=pend)
    return (y.reshape(b, l, d), jnp.stack(new_delta), jnp.stack(new_conv),
            k_new.reshape(b, l, n_hb, HEAD), v_new.reshape(b, l, n_hb, HEAD),
            logf.reshape(b, l, n_hb))


def kernel(x_prompt, x_sample, cache_k, cache_v, cache_logf, state_delta, state_conv, page_table,
           norm_mix_g, norm_mlp_g, w_up, w_down, a_w_in, a_w_conv, a_log, a_dt_bias, a_o_norm_g,
           a_w_out, kv_norm_g, w_k, w_v, w_f, b_f, b_w_q, b_w_o, final_norm_g):
    conv_dim = a_w_conv.shape[-1]
    v_dim = a_w_out.shape[1]
    wts = _Weights(dict(w_up=w_up, w_down=w_down, w_in=a_w_in, w_out=a_w_out, w_k=w_k, w_v=w_v,
                        w_q=b_w_q, w_o=b_w_o), w_f, conv_dim + v_dim)
    prm = dict(norm_mix_g=norm_mix_g, norm_mlp_g=norm_mlp_g, a_w_conv=a_w_conv, a_log=a_log,
               a_dt_bias=a_dt_bias, a_o_norm_g=a_o_norm_g, kv_norm_g=kv_norm_g, b_f=b_f,
               b_w_q=b_w_q, final_norm_g=final_norm_g)
    n_a = a_log.shape[0]
    n_p = x_prompt.shape[0]
    n_h = a_log.shape[1]
    zero_conv = jnp.zeros((n_a, n_p, CONV_W - 1, conv_dim), state_conv.dtype)
    zero_delta = jnp.zeros((n_a, n_p, n_h, HEAD, HEAD), state_delta.dtype)
    y_s, delta_s, conv_s, k_s, v_s, logf_s = _trunk(
        x_sample, state_conv, state_delta, (cache_k, cache_v, cache_logf, page_table), wts, prm)
    y_p, delta_p, conv_p, k_p, v_p, logf_p = _trunk(x_prompt, zero_conv, zero_delta, None, wts, prm)
    return (y_p, y_s, delta_p, conv_p, k_p, v_p, logf_p, delta_s, conv_s, k_s, v_s, logf_s)
```
